```python
import jax, jax.numpy as jnp
from jax import lax
import numpy as np

D_MODEL = 1024
BATCH = 8
SEQ = 2048
DEPTH = 4
DEC_BATCH = 32
DEC_SEQ = 4
PAST_LEN = 16384
PAGE_SIZE = 128

HEAD_DIM = 64
ROT_DIM = HEAD_DIM // 4
ROPE_THETA = 500000.0
Q_BLOCK = 128
SEL_Q_BLOCK = 32
RMS_EPS = 1e-6
NSA_HEADS = 8
NSA_KV_HEADS = 2
NSA_GROUP = NSA_HEADS // NSA_KV_HEADS
CMP_BLOCK = 32
SEL_BLOCK = 64
N_SEL = 16
NSA_WINDOW = 512
MLA_HEADS = 8
Q_LORA = 384
KV_LORA = 256
QK_NOPE = 64
QK_ROPE = 32
V_DIM = 64
DIL_PATTERNS = ((128, 1), (512, 4), (2048, 16))
N_DIL = len(DIL_PATTERNS)
DIL_HEADS = 8
D_FF = ((-(-8 * D_MODEL // 3) + 255) // 256) * 256
PLE_DIM = 256
N_AB = (DEPTH + 1) // 2
N_C = DEPTH // 2
AB_SPLITS = (NSA_HEADS * HEAD_DIM, 6 * NSA_KV_HEADS * HEAD_DIM, 3 * NSA_HEADS, Q_LORA, KV_LORA, QK_ROPE)
AB_IN = sum(AB_SPLITS)
AB_OUT = NSA_HEADS * HEAD_DIM + MLA_HEADS * V_DIM
C_IN = N_DIL * 3 * DIL_HEADS * HEAD_DIM
C_OUT = DIL_HEADS * HEAD_DIM

kernel_name = 'nsa_mla_dilated_hybrid_step'


def rmsnorm(x, g):
    xf = x.astype(jnp.float32)
    y = xf * lax.rsqrt(jnp.mean(xf * xf, axis=-1, keepdims=True) + RMS_EPS)
    return (y * g.astype(jnp.float32)).astype(x.dtype)


def rope(x, pos, rot_dim):
    half = rot_dim // 2
    inv = jnp.power(ROPE_THETA, -jnp.arange(half, dtype=jnp.float32) / half)
    ang = pos.astype(jnp.float32)[:, None] * inv[None]
    ang = ang.reshape((1, ang.shape[0]) + (1,) * (x.ndim - 3) + (half,))
    cos, sin = jnp.cos(ang), jnp.sin(ang)
    xr = x[..., :rot_dim].astype(jnp.float32)
    x1, x2 = xr[..., :half], xr[..., half:]
    rot = jnp.concatenate([x1 * cos - x2 * sin, x2 * cos + x1 * sin], axis=-1).astype(x.dtype)
    return jnp.concatenate([rot, x[..., rot_dim:]], axis=-1)


def masked_softmax(s, mask):
    s = jnp.where(mask, s.astype(jnp.float32), -jnp.inf)
    m = jnp.max(s, axis=-1, keepdims=True)
    m = jnp.where(jnp.isfinite(m), m, 0.0)
    e = jnp.where(mask, jnp.exp(s - m), 0.0)
    den = jnp.sum(e, axis=-1, keepdims=True)
    p = e / jnp.maximum(den, 1e-30)
    return p, (jnp.log(den) + m)[..., 0]


def masked_attn(q, k, v, mask):
    s = jnp.einsum('...qgrd,...kgd->...grqk', q, k, preferred_element_type=jnp.float32) * (q.shape[-1] ** -0.5)
    p, lse = masked_softmax(s, mask)
    o = jnp.einsum('...grqk,...kgd->...qgrd', p.astype(v.dtype), v)
    return o, jnp.moveaxis(lse, -1, -3)


def banded_attn(q, k, v, n_back):
    B, L = q.shape[:2]
    qb = min(Q_BLOCK, L)
    nb = -(-L // qb)
    lp = nb * qb
    q = jnp.pad(q, ((0, 0), (0, lp - L), (0, 0), (0, 0), (0, 0)))
    kpad = ((0, 0), (n_back, lp - L), (0, 0), (0, 0))
    k, v = jnp.pad(k, kpad), jnp.pad(v, kpad)
    kidx = jnp.arange(nb)[:, None] * qb + jnp.arange(qb + n_back)[None]
    kpos = (kidx - n_back)[:, None, :]
    qpos = jnp.arange(lp).reshape(nb, qb)[:, :, None]
    mask = (kpos <= qpos) & (kpos >= qpos - n_back) & (kpos >= 0)
    o, lse = masked_attn(q.reshape((B, nb, qb) + q.shape[2:]), k[:, kidx], v[:, kidx], mask[None, :, None, None])
    o = o.reshape((B, lp) + o.shape[3:])[:, :L]
    lse = lse.reshape((B, lp) + lse.shape[3:])[:, :L]
    return o, lse


def over_query_blocks(fn, qb, *qs):
    T = qs[0].shape[1]
    qb = min(qb, T)
    nb = -(-T // qb)
    pad = nb * qb - T

    def split(a):
        a = jnp.pad(a, [(0, 0), (0, pad)] + [(0, 0)] * (a.ndim - 2))
        return jnp.moveaxis(a.reshape((a.shape[0], nb, qb) + a.shape[2:]), 1, 0)

    out = lax.map(lambda args: fn(*args), tuple(split(a) for a in qs))

    def merge(o):
        o = jnp.moveaxis(o, 0, 1)
        return o.reshape((o.shape[0], nb * qb) + o.shape[3:])[:, :T]

    return jax.tree_util.tree_map(merge, out)


def split_cols(z, sizes):
    return jnp.split(z, np.cumsum(sizes)[:-1].tolist(), axis=-1)


def nsa_compress(rows, pe, phi):
    B, S, G, d = rows.shape
    n = S // CMP_BLOCK
    blk = rows[:, :n * CMP_BLOCK].reshape(B, n, CMP_BLOCK, G, d) + pe
    return jnp.einsum('bngd,gde->bnge', jnp.mean(blk, axis=2), phi)


def nsa_mixer(q, kv, gates, q_pos, pe_k, pe_v, phi_k, phi_v, past, win_past):
    B, T = q.shape[:2]
    G, R, dh = NSA_KV_HEADS, NSA_GROUP, HEAD_DIM
    scale = dh ** -0.5
    q_cmp = q.reshape(B, T, G, R, dh)
    q_rot = rope(q, q_pos, ROT_DIM).reshape(B, T, G, R, dh)
    k_win = rope(kv[:, :, 4], q_pos, ROT_DIM)
    rows = jnp.stack([kv[:, :, 0], kv[:, :, 1], rope(kv[:, :, 2], q_pos, ROT_DIM), kv[:, :, 3]], axis=2)
    win_rows = jnp.stack([k_win, kv[:, :, 5]], axis=2)
    all_rows = rows if past is None else jnp.concatenate([past, rows], axis=1)
    S = all_rows.shape[1]
    P = S - T
    kc = nsa_compress(all_rows[:, :, 0], pe_k, phi_k)
    vc = nsa_compress(all_rows[:, :, 1], pe_v, phi_v)
    ncb = kc.shape[1]
    cmask = (jnp.arange(1, ncb + 1) * CMP_BLOCK - 1)[None, :] <= q_pos[:, None]
    s = jnp.einsum('btgrd,bngd->bgrtn', q_cmp, kc, preferred_element_type=jnp.float32) * scale
    p_cmp, _ = masked_softmax(s, cmask)
    o_cmp = jnp.einsum('bgrtn,bngd->btgrd', p_cmp.astype(vc.dtype), vc)
    ratio = SEL_BLOCK // CMP_BLOCK
    nsb = -(-S // SEL_BLOCK)
    imp = jnp.pad(jnp.sum(p_cmp, axis=2), ((0, 0), (0, 0), (0, 0), (0, nsb * ratio - ncb)))
    imp = imp.reshape(B, G, T, nsb, ratio).sum(-1)
    blk = jnp.arange(nsb)[None, :]
    eligible = blk * SEL_BLOCK <= q_pos[:, None]
    forced = (blk == 0) | (blk == (q_pos // SEL_BLOCK)[:, None])
    imp = jnp.where(forced, jnp.inf, jnp.where(eligible, imp, -jnp.inf))
    top, idx = lax.top_k(imp, min(N_SEL, nsb))
    idx = jnp.transpose(idx, (0, 2, 1, 3))
    valid = jnp.transpose(top > -jnp.inf, (0, 2, 1, 3))

    def to_blocks(a):
        a = jnp.pad(a, ((0, 0), (0, nsb * SEL_BLOCK - S), (0, 0), (0, 0)))
        return jnp.moveaxis(a.reshape(B, nsb, SEL_BLOCK, G, dh), 3, 1)

    ks, vs = to_blocks(all_rows[:, :, 2]), to_blocks(all_rows[:, :, 3])
    bi = jnp.arange(B)[:, None, None, None]
    gi = jnp.arange(G)[None, None, :, None]
    offs = jnp.arange(SEL_BLOCK)

    def sel_attend(qb_, idx_b, valid_b, pos_b):
        tq, n = idx_b.shape[1], idx_b.shape[3]
        kb = ks[bi, gi, idx_b].reshape(B, tq, G, n * SEL_BLOCK, dh)
        vb = vs[bi, gi, idx_b].reshape(B, tq, G, n * SEL_BLOCK, dh)
        kpos = idx_b[..., None] * SEL_BLOCK + offs
        mask = (valid_b[..., None] & (kpos <= pos_b[0][None, :, None, None, None])).reshape(B, tq, G, 1, n * SEL_BLOCK)
        sc = jnp.einsum('btgrd,btgkd->btgrk', qb_, kb, preferred_element_type=jnp.float32) * scale
        p, _ = masked_softmax(sc, mask)
        return jnp.einsum('btgrk,btgkd->btgrd', p.astype(vb.dtype), vb)

    o_slc = over_query_blocks(sel_attend, SEL_Q_BLOCK, q_rot, idx, valid, q_pos[None])
    if win_past is None:
        o_win, _ = banded_attn(q_rot, k_win, kv[:, :, 5], NSA_WINDOW - 1)
    else:
        Lw = win_past.shape[1]
        kw = jnp.concatenate([win_past[:, :, 0], k_win], axis=1)
        vw = jnp.concatenate([win_past[:, :, 1], kv[:, :, 5]], axis=1)
        kpos = (P - Lw) + jnp.arange(Lw + T)
        wmask = (kpos[None] <= q_pos[:, None]) & (kpos[None] >= q_pos[:, None] - (NSA_WINDOW - 1))
        o_win, _ = masked_attn(q_rot, kw, vw, wmask)
    g = jax.nn.sigmoid(gates)[..., None]
    hs = (B, T, NSA_HEADS, dh)
    o = g[:, :, 0] * o_cmp.reshape(hs) + g[:, :, 1] * o_slc.reshape(hs) + g[:, :, 2] * o_win.reshape(hs)
    return o.reshape(B, T, NSA_HEADS * dh), rows, win_rows


def mla_mixer(q_lat, kv_lat, kpe_raw, q_pos, g_q, g_kv, w_uq, w_uk, w_uv, past):
    B, T = q_lat.shape[:2]
    qf = jnp.einsum('btc,ce->bte', rmsnorm(q_lat, g_q), w_uq).reshape(B, T, MLA_HEADS, QK_NOPE + QK_ROPE)
    q_nope = qf[..., :QK_NOPE]
    q_pe = rope(qf[..., QK_NOPE:], q_pos, QK_ROPE)
    new_rows = jnp.concatenate([rmsnorm(kv_lat, g_kv), rope(kpe_raw, q_pos, QK_ROPE)], axis=-1)
    rows = new_rows if past is None else jnp.concatenate([past, new_rows], axis=1)
    ckv, kpe = rows[..., :KV_LORA], rows[..., KV_LORA:]
    kpos = jnp.arange(rows.shape[1])
    q_abs = jnp.einsum('bthn,chn->bthc', q_nope, w_uk)
    scale = (QK_NOPE + QK_ROPE) ** -0.5

    def attend(qa, qp, pos):
        s = (jnp.einsum('bthc,bsc->bhts', qa, ckv, preferred_element_type=jnp.float32)
             + jnp.einsum('bthr,bsr->bhts', qp, kpe, preferred_element_type=jnp.float32)) * scale
        p, _ = masked_softmax(s, kpos[None, :] <= pos[0][:, None])
        return jnp.einsum('bhts,bsc->bthc', p.astype(ckv.dtype), ckv)

    o_lat = over_query_blocks(attend, Q_BLOCK, q_abs, q_pe, q_pos[None])
    o = jnp.einsum('bthc,chv->bthv', o_lat, w_uv)
    return o.reshape(B, T, MLA_HEADS * V_DIM), new_rows


def dilated_prompt(q, k, v, window, dil):
    B, T, H, dh = q.shape
    L = T // dil

    def to_sub(a):
        a = jnp.moveaxis(a.reshape((B, L, dil) + a.shape[2:]), 2, 1)
        return a.reshape((B * dil, L) + a.shape[3:])

    def from_sub(a):
        a = jnp.moveaxis(a.reshape((B, dil, L) + a.shape[2:]), 1, 2)
        return a.reshape((B, T) + a.shape[3:])

    o, lse = banded_attn(to_sub(q)[:, :, :, None], to_sub(k), to_sub(v), window // dil)
    return from_sub(o[:, :, :, 0]), from_sub(lse[..., 0])


def dilated_sample(q, k, v, buf, window, dil):
    Lb, T = buf.shape[1], q.shape[1]
    kc = jnp.concatenate([buf[:, :, 0], k], axis=1)
    vc = jnp.concatenate([buf[:, :, 1], v], axis=1)
    j = jnp.arange(window // dil + 1)
    idx = Lb + jnp.arange(T)[:, None] - j[None] * dil
    ok = idx >= 0
    idx = jnp.maximum(idx, 0)
    kg, vg = kc[:, idx], vc[:, idx]
    s = jnp.einsum('bthd,btjhd->bhtj', q, kg, preferred_element_type=jnp.float32) * (q.shape[-1] ** -0.5)
    p, lse = masked_softmax(s, ok[None, None])
    o = jnp.einsum('bhtj,btjhd->bthd', p.astype(vg.dtype), vg)
    return o, jnp.transpose(lse, (0, 2, 1))


def dilated_mixer(qkv, q_pos, bufs):
    B, T = qkv.shape[:2]
    outs, lses, rows = [], [], []
    for g, (window, dil) in enumerate(DIL_PATTERNS):
        q = rope(qkv[:, :, g, 0], q_pos, ROT_DIM)
        k = rope(qkv[:, :, g, 1], q_pos, ROT_DIM)
        v = qkv[:, :, g, 2]
        if bufs is None:
            o, lse = dilated_prompt(q, k, v, window, dil)
        else:
            o, lse = dilated_sample(q, k, v, bufs[g], window, dil)
        outs.append(o)
        lses.append(lse)
        rows.append(jnp.stack([k, v], axis=2))
    w = jax.nn.softmax(jnp.stack(lses, 0).astype(jnp.float32), axis=0)[..., None]
    o = jnp.sum(w * jnp.stack(outs, 0).astype(jnp.float32), axis=0)
    return o.reshape(B, T, DIL_HEADS * HEAD_DIM).astype(qkv.dtype), rows


def swiglu(x, w_gate_up, w_down):
    gate, up = jnp.split(x @ w_gate_up, 2, axis=-1)
    return (jax.nn.silu(gate) * up) @ w_down


def trunk(x, ple, pos, prm, paged, bufs):
    B, T, _ = x.shape
    prompt = paged is None

    def keep(r, w):
        return r[:, max(T - w, 0):] if prompt else r

    nsa_rows, mla_rows, win_rows = [], [], []
    dil_rows = [[] for _ in DIL_PATTERNS]
    h = x
    for i in range(DEPTH):
        l = i // 2
        hn = rmsnorm(h, prm['norm_mix'][i])
        if i % 2 == 0:
            q_a, kv_a, g_a, q_lat, kv_lat, kpe = split_cols(hn @ prm['w_in_ab'][l], AB_SPLITS)
            past_nsa = past_mla = past_win = None
            if not prompt:
                pool_nsa, pool_mla, pt = paged
                past_nsa = pool_nsa[pt, :, l].reshape((B, -1) + pool_nsa.shape[3:])
                past_mla = pool_mla[pt, :, l].reshape(B, -1, pool_mla.shape[-1])
                past_win = bufs[0][l]
            o_a, r_nsa, r_win = nsa_mixer(q_a.reshape(B, T, NSA_HEADS, HEAD_DIM),
                                          kv_a.reshape(B, T, 6, NSA_KV_HEADS, HEAD_DIM),
                                          g_a.reshape(B, T, 3, NSA_HEADS), pos,
                                          prm['nsa_pe_k'][l], prm['nsa_pe_v'][l], prm['nsa_phi_k'][l], prm['nsa_phi_v'][l],
                                          past_nsa, past_win)
            o_b, r_mla = mla_mixer(q_lat, kv_lat, kpe, pos, prm['mla_q_norm'][l], prm['mla_kv_norm'][l],
                                   prm['mla_w_uq'][l], prm['mla_w_uk'][l], prm['mla_w_uv'][l], past_mla)
            mix = jnp.concatenate([o_a, o_b], axis=-1) @ prm['w_out_ab'][l]
            nsa_rows.append(r_nsa)
            mla_rows.append(r_mla)
            win_rows.append(keep(r_win, NSA_WINDOW))
        else:
            qkv = (hn @ prm['w_in_c'][l]).reshape(B, T, N_DIL, 3, DIL_HEADS, HEAD_DIM)
            past = None if prompt else [bufs[1 + g][l] for g in range(N_DIL)]
            o_c, rows = dilated_mixer(qkv, pos, past)
            mix = o_c @ prm['w_out_c'][l]
            for g, (window, _) in enumerate(DIL_PATTERNS):
                dil_rows[g].append(keep(rows[g], window))
        h = h + mix
        h = h + swiglu(rmsnorm(h, prm['norm_ffn'][i]), prm['w_gate_up'][i], prm['w_down'][i])
        gate = jax.nn.sigmoid(rmsnorm(h, prm['norm_ple'][i]) @ prm['w_ple_gate'][i])
        h = h + gate * (ple[i] @ prm['w_ple_proj'][i])
    y = rmsnorm(h, prm['norm_final'])
    return y, jnp.stack(nsa_rows, 2), jnp.stack(mla_rows, 2), jnp.stack(win_rows, 0), [jnp.stack(r, 0) for r in dil_rows]


def setup_inputs(seed: int = 0) -> dict:
    key = jax.random.key(seed)
    ks = iter(jax.random.split(key, 40))

    def nrm(shape, scale=1.0):
        return jax.random.normal(next(ks), shape, jnp.float32) * scale

    def gain(shape):
        return 1.0 + 0.1 * jax.random.normal(next(ks), shape, jnp.float32)

    n_pages = PAST_LEN // PAGE_SIZE
    used = DEC_BATCH * n_pages
    n_pool = used + max(1, used // 4)
    page_table = jax.random.permutation(next(ks), n_pool)[:used].reshape(DEC_BATCH, n_pages).astype(jnp.int32)
    return {
        'x_prompt': nrm((BATCH, SEQ, D_MODEL)),
        'x_sample': nrm((DEC_BATCH, DEC_SEQ, D_MODEL)),
        'cache_nsa_kv': nrm((n_pool, PAGE_SIZE, N_AB, 4, NSA_KV_HEADS, HEAD_DIM)),
        'cache_mla': nrm((n_pool, PAGE_SIZE, N_AB, KV_LORA + QK_ROPE)),
        'cache_nsa_win': nrm((N_AB, DEC_BATCH, min(NSA_WINDOW, PAST_LEN), 2, NSA_KV_HEADS, HEAD_DIM)),
        'cache_dil0': nrm((N_C, DEC_BATCH, min(DIL_PATTERNS[0][0], PAST_LEN), 2, DIL_HEADS, HEAD_DIM)),
        'cache_dil1': nrm((N_C, DEC_BATCH, min(DIL_PATTERNS[1][0], PAST_LEN), 2, DIL_HEADS, HEAD_DIM)),
        'cache_dil2': nrm((N_C, DEC_BATCH, min(DIL_PATTERNS[2][0], PAST_LEN), 2, DIL_HEADS, HEAD_DIM)),
        'page_table': page_table,
        'p_prompt': nrm((DEPTH, BATCH, SEQ, PLE_DIM)),
        'p_sample': nrm((DEPTH, DEC_BATCH, DEC_SEQ, PLE_DIM)),
        'w_in_ab': nrm((N_AB, D_MODEL, AB_IN), D_MODEL ** -0.5),
        'w_out_ab': nrm((N_AB, AB_OUT, D_MODEL), AB_OUT ** -0.5),
        'nsa_pe_k': nrm((N_AB, CMP_BLOCK, NSA_KV_HEADS, HEAD_DIM), 0.1),
        'nsa_pe_v': nrm((N_AB, CMP_BLOCK, NSA_KV_HEADS, HEAD_DIM), 0.1),
        'nsa_phi_k': nrm((N_AB, NSA_KV_HEADS, HEAD_DIM, HEAD_DIM), HEAD_DIM ** -0.5),
        'nsa_phi_v': nrm((N_AB, NSA_KV_HEADS, HEAD_DIM, HEAD_DIM), HEAD_DIM ** -0.5),
        'mla_q_norm': gain((N_AB, Q_LORA)),
        'mla_kv_norm': gain((N_AB, KV_LORA)),
        'mla_w_uq': nrm((N_AB, Q_LORA, MLA_HEADS * (QK_NOPE + QK_ROPE)), Q_LORA ** -0.5),
        'mla_w_uk': nrm((N_AB, KV_LORA, MLA_HEADS, QK_NOPE), KV_LORA ** -0.5),
        'mla_w_uv': nrm((N_AB, KV_LORA, MLA_HEADS, V_DIM), KV_LORA ** -0.5),
        'w_in_c': nrm((N_C, D_MODEL, C_IN), D_MODEL ** -0.5),
        'w_out_c': nrm((N_C, C_OUT, D_MODEL), C_OUT ** -0.5),
        'norm_mix': gain((DEPTH, D_MODEL)),
        'norm_ffn': gain((DEPTH, D_MODEL)),
        'w_gate_up': nrm((DEPTH, D_MODEL, 2 * D_FF), D_MODEL ** -0.5),
        'w_down': nrm((DEPTH, D_FF, D_MODEL), D_FF ** -0.5),
        'norm_ple': gain((DEPTH, D_MODEL)),
        'w_ple_gate': nrm((DEPTH, D_MODEL, D_MODEL), D_MODEL ** -0.5),
        'w_ple_proj': nrm((DEPTH, PLE_DIM, D_MODEL), PLE_DIM ** -0.5),
        'norm_final': gain((D_MODEL,)),
    }


def reference(x_prompt, x_sample, cache_nsa_kv, cache_mla, cache_nsa_win, cache_dil0, cache_dil1, cache_dil2,
              page_table, p_prompt, p_sample, w_in_ab, w_out_ab, nsa_pe_k, nsa_pe_v, nsa_phi_k, nsa_phi_v,
              mla_q_norm, mla_kv_norm, mla_w_uq, mla_w_uk, mla_w_uv, w_in_c, w_out_c, norm_mix, norm_ffn,
              w_gate_up, w_down, norm_ple, w_ple_gate, w_ple_proj, norm_final):
    prm = dict(w_in_ab=w_in_ab, w_out_ab=w_out_ab, nsa_pe_k=nsa_pe_k, nsa_pe_v=nsa_pe_v, nsa_phi_k=nsa_phi_k,
               nsa_phi_v=nsa_phi_v, mla_q_norm=mla_q_norm, mla_kv_norm=mla_kv_norm, mla_w_uq=mla_w_uq,
               mla_w_uk=mla_w_uk, mla_w_uv=mla_w_uv, w_in_c=w_in_c, w_out_c=w_out_c, norm_mix=norm_mix,
               norm_ffn=norm_ffn, w_gate_up=w_gate_up, w_down=w_down, norm_ple=norm_ple,
               w_ple_gate=w_ple_gate, w_ple_proj=w_ple_proj, norm_final=norm_final)
    pos_p = jnp.arange(x_prompt.shape[1], dtype=jnp.int32)
    past_len = page_table.shape[1] * cache_nsa_kv.shape[1]
    pos_s = past_len + jnp.arange(x_sample.shape[1], dtype=jnp.int32)
    y_p, nsa_p, mla_p, win_p, dil_p = trunk(x_prompt, p_prompt, pos_p, prm, None, None)
    y_s, nsa_s, mla_s, win_s, dil_s = trunk(x_sample, p_sample, pos_s, prm,
                                            (cache_nsa_kv, cache_mla, page_table),
                                            (cache_nsa_win, cache_dil0, cache_dil1, cache_dil2))
    return (y_p, y_s, nsa_p, nsa_s, mla_p, mla_s, win_p, win_s,
            dil_p[0], dil_s[0], dil_p[1], dil_s[1], dil_p[2], dil_s[2])
```

```python
import functools

import numpy as np
import jax
import jax.numpy as jnp
from jax import lax
from jax.experimental import pallas as pl
from jax.experimental.pallas import tpu as pltpu

F32 = jnp.float32
BF16 = jnp.bfloat16

D_MODEL = 1024
HEAD_DIM = 64
ROT_DIM = HEAD_DIM // 4
ROPE_THETA = 500000.0
RMS_EPS = 1e-6
NSA_HEADS = 8
NSA_KV_HEADS = 2
NSA_GROUP = NSA_HEADS // NSA_KV_HEADS
CMP_BLOCK = 32
SEL_BLOCK = 64
N_SEL = 16
NSA_WINDOW = 512
MLA_HEADS = 8
Q_LORA = 384
KV_LORA = 256
QK_NOPE = 64
QK_ROPE = 32
V_DIM = 64
MLA_ROW = KV_LORA + QK_ROPE
DIL_PATTERNS = ((128, 1), (512, 4), (2048, 16))
N_DIL = len(DIL_PATTERNS)
DIL_HEADS = 8
DIL_BACK = 128
D_FF = ((-(-8 * D_MODEL // 3) + 255) // 256) * 256
PLE_DIM = 256
AB_SPLITS = (NSA_HEADS * HEAD_DIM, 6 * NSA_KV_HEADS * HEAD_DIM, 3 * NSA_HEADS, Q_LORA, KV_LORA, QK_ROPE)

LANES = 128
NEG = -1e30
V7X_VMEM_LIMIT = 56 * 1024 * 1024
ROW_TILE = 256
FF_CHUNK = 256
NSA_TQ = 256
NSA_TK = 512
MLA_TQ = 512
MLA_TK = 512
PAGES_PER_STEP = 8


def _dot(a, b):
    return jnp.dot(a.astype(BF16), b.astype(BF16), preferred_element_type=F32)


def _dot_nt(a, b):
    return lax.dot_general(a.astype(BF16), b.astype(BF16), (((1,), (1,)), ((), ())),
                           preferred_element_type=F32)


def _rms(x, g):
    return x * lax.rsqrt(jnp.mean(x * x, axis=-1, keepdims=True) + RMS_EPS) * g


def _tile_lanes(a, reps):
    return a if reps == 1 else jnp.concatenate([a] * reps, axis=1)


def _rope(x, cos, sin_up, sin_dn, half):
    w = x.shape[-1]
    reps = w // LANES
    return (x * _tile_lanes(cos, reps)
            + pltpu.roll(x, half, 1) * _tile_lanes(sin_up, reps)
            + pltpu.roll(x, w - half, 1) * _tile_lanes(sin_dn, reps))


def _rope_tables(pos, rot_dim, period, offset):
    half = rot_dim // 2
    inv = jnp.power(ROPE_THETA, -jnp.arange(half, dtype=F32) / half)
    ang = pos.astype(F32)[:, None] * inv[None]
    cos, sin = jnp.cos(ang), jnp.sin(ang)
    lane = np.arange(LANES) % period - offset
    lo = (lane >= 0) & (lane < half)
    hi = (lane >= half) & (lane < rot_dim)
    j = np.clip(np.where(hi, lane - half, lane), 0, half - 1)
    cos_t = jnp.where(lo | hi, cos[:, j], 1.0)
    sin_up = jnp.where(hi, sin[:, j], 0.0)
    sin_dn = jnp.where(lo, -sin[:, j], 0.0)
    return cos_t, sin_up, sin_dn


def _softmax_cols(pieces):
    m = None
    for s, mask in pieces:
        mm = jnp.max(jnp.where(mask, s, NEG), axis=0, keepdims=True)
        m = mm if m is None else jnp.maximum(m, mm)
    es = [jnp.where(mask, jnp.exp(s - m), 0.0) for s, mask in pieces]
    den = sum(jnp.sum(e, axis=0, keepdims=True) for e in es)
    inv = 1.0 / jnp.maximum(den, 1e-30)
    return [e * inv for e in es], den, m


def _softmax_rows(pieces):
    m = None
    for s, mask in pieces:
        mm = jnp.max(jnp.where(mask, s, NEG), axis=1, keepdims=True)
        m = mm if m is None else jnp.maximum(m, mm)
    es = [jnp.where(mask, jnp.exp(s - m), 0.0) for s, mask in pieces]
    den = sum(jnp.sum(e, axis=1, keepdims=True) for e in es)
    inv = 1.0 / jnp.maximum(den, 1e-30)
    return [e * inv for e in es], den, m


def _flash_update(carry, s_t, mask, v_t):
    m, l, acc = carry
    m_new = jnp.maximum(m, jnp.max(jnp.where(mask, s_t, NEG), axis=0, keepdims=True))
    p = jnp.where(mask, jnp.exp(s_t - m_new), 0.0)
    alpha = jnp.exp(m - m_new)
    l_new = alpha * l + jnp.sum(p, axis=0, keepdims=True)
    acc_new = alpha * acc + jnp.dot(v_t, p.astype(BF16), preferred_element_type=F32)
    return m_new, l_new, acc_new


def _params(n_grid):
    return pltpu.CompilerParams(dimension_semantics=("arbitrary",) * n_grid,
                                vmem_limit_bytes=V7X_VMEM_LIMIT)


def _resident(shape):
    nd = len(shape)
    return pl.BlockSpec(shape, lambda *_: (0,) * nd, pipeline_mode=pl.Buffered(1))


def _rows(width, tile):
    return pl.BlockSpec((tile, width), lambda i: (i, 0))


def _pre_ab_kernel(h_ref, gm_ref, w_ref, gq_ref, gkv_ref, wuq_ref,
                   c64_ref, u64_ref, d64_ref, cm_ref, um_ref, dm_ref,
                   qc_ref, qr_ref, rows_ref, win_ref, qf_ref, mla_ref, misc_ref):
    t64 = (c64_ref[...], u64_ref[...], d64_ref[...])
    tm_ = (cm_ref[...], um_ref[...], dm_ref[...])
    hn = _rms(h_ref[...], gm_ref[...])
    z = _dot(hn, w_ref[...])
    nq = NSA_HEADS * LANES
    qc = z[:, 0:nq] * (HEAD_DIM ** -0.5)
    qc_ref[...] = qc
    qr_ref[...] = _rope(qc, *t64, ROT_DIM // 2)
    kv = z[:, nq:nq + 768]
    rows_ref[:, 0:256] = kv[:, 0:256]
    rows_ref[:, 256:384] = _rope(kv[:, 256:384], *t64, ROT_DIM // 2)
    rows_ref[:, 384:512] = kv[:, 384:512]
    win_ref[:, 0:128] = _rope(kv[:, 512:640], *t64, ROT_DIM // 2)
    win_ref[:, 128:256] = kv[:, 640:768]
    o = nq + 768
    qn = _rms(z[:, o:o + Q_LORA], gq_ref[...])
    qf_ref[...] = _rope(_dot(qn, wuq_ref[...]), *tm_, QK_ROPE // 2)
    o += Q_LORA
    ckv = _rms(z[:, o:o + KV_LORA], gkv_ref[...])
    o += KV_LORA
    misc = _rope(z[:, o:o + LANES], *tm_, QK_ROPE // 2)
    mla_ref[:, 0:KV_LORA] = ckv
    mla_ref[:, KV_LORA:MLA_ROW] = misc[:, 0:QK_ROPE]
    misc_ref[...] = misc


def _pre_ab(h, gm, w, gq, gkv, wuq, tabs64, tabsm, tile):
    n = h.shape[0]
    t_tab = tabs64[0].shape[0]
    nt = t_tab // tile
    tab = pl.BlockSpec((tile, LANES), lambda i: (i % nt, 0))
    widths = (NSA_HEADS * LANES, NSA_HEADS * LANES, 512, 256, MLA_HEADS * LANES, MLA_ROW, LANES)
    return pl.pallas_call(
        _pre_ab_kernel,
        grid=(n // tile,),
        in_specs=[_rows(D_MODEL, tile), _resident(gm.shape), _resident(w.shape), _resident(gq.shape),
                  _resident(gkv.shape), _resident(wuq.shape)] + [tab] * 6,
        out_specs=[_rows(wd, tile) for wd in widths],
        out_shape=[jax.ShapeDtypeStruct((n, wd), F32) for wd in widths],
        compiler_params=_params(1),
        name="pre_ab",
    )(h, gm, w, gq, gkv, wuq, *tabs64, *tabsm)


def _mla_kv_kernel(mla_ref, misc_ref, w_ref, k_ref, v_ref):
    x = jnp.concatenate([mla_ref[:, 0:KV_LORA], misc_ref[...]], axis=1)
    z = _dot(x, w_ref[...])
    k_ref[...] = z[:, 0:MLA_HEADS * LANES]
    v_ref[...] = z[:, MLA_HEADS * LANES:]


def _mla_kv(mla, misc, w, tile):
    n = mla.shape[0]
    return pl.pallas_call(
        _mla_kv_kernel,
        grid=(n // tile,),
        in_specs=[_rows(MLA_ROW, tile), _rows(LANES, tile), _resident(w.shape)],
        out_specs=[_rows(MLA_HEADS * LANES, tile), _rows(MLA_HEADS * V_DIM, tile)],
        out_shape=[jax.ShapeDtypeStruct((n, MLA_HEADS * LANES), F32),
                   jax.ShapeDtypeStruct((n, MLA_HEADS * V_DIM), F32)],
        compiler_params=_params(1),
        name="mla_kv",
    )(mla, misc, w)


def _pre_c_kernel(h_ref, gm_ref, w_ref, c64_ref, u64_ref, d64_ref, *out_refs):
    t64 = (c64_ref[...], u64_ref[...], d64_ref[...])
    hn = _rms(h_ref[...], gm_ref[...])
    z = _dot(hn, w_ref[...])
    hw = DIL_HEADS * HEAD_DIM
    for g in range(N_DIL):
        o = g * 3 * hw
        out_refs[g][...] = _rope(z[:, o:o + hw] * (HEAD_DIM ** -0.5), *t64, ROT_DIM // 2)
        out_refs[N_DIL + g][:, 0:hw] = _rope(z[:, o + hw:o + 2 * hw], *t64, ROT_DIM // 2)
        out_refs[N_DIL + g][:, hw:2 * hw] = z[:, o + 2 * hw:o + 3 * hw]


def _pre_c(h, gm, w, tabs64, tile):
    n = h.shape[0]
    nt = tabs64[0].shape[0] // tile
    tab = pl.BlockSpec((tile, LANES), lambda i: (i % nt, 0))
    hw = DIL_HEADS * HEAD_DIM
    widths = (hw,) * N_DIL + (2 * hw,) * N_DIL
    return pl.pallas_call(
        _pre_c_kernel,
        grid=(n // tile,),
        in_specs=[_rows(D_MODEL, tile), _resident(gm.shape), _resident(w.shape)] + [tab] * 3,
        out_specs=[_rows(wd, tile) for wd in widths],
        out_shape=[jax.ShapeDtypeStruct((n, wd), F32) for wd in widths],
        compiler_params=_params(1),
        name="pre_c",
    )(h, gm, w, *tabs64)


def _post_kernel(final, h_ref, a_ref, wout_ref, gffn_ref, wgu_ref, wd_ref, gple_ref, wpg_ref,
                 ple_ref, wpp_ref, gfin_ref, o_ref, *y_ref):
    h1 = h_ref[...] + _dot(a_ref[...], wout_ref[...])
    xn = _rms(h1, gffn_ref[...]).astype(BF16)
    acc = jnp.zeros(h1.shape, F32)
    for c in range(D_FF // FF_CHUNK):
        lo = c * FF_CHUNK
        g = jnp.dot(xn, wgu_ref[:, lo:lo + FF_CHUNK], preferred_element_type=F32)
        u = jnp.dot(xn, wgu_ref[:, D_FF + lo:D_FF + lo + FF_CHUNK], preferred_element_type=F32)
        a = g * jax.nn.sigmoid(g) * u
        acc = acc + jnp.dot(a.astype(BF16), wd_ref[lo:lo + FF_CHUNK, :], preferred_element_type=F32)
    h2 = h1 + acc
    gate = jax.nn.sigmoid(_dot(_rms(h2, gple_ref[...]), wpg_ref[...]))
    h3 = h2 + gate * _dot(ple_ref[...], wpp_ref[...])
    o_ref[...] = h3
    if final:
        y_ref[0][...] = _rms(h3, gfin_ref[...])


def _post(h, a, wout, gffn, wgu, wd, gple, wpg, ple, wpp, gfin, final, tile):
    n = h.shape[0]
    n_out = 2 if final else 1
    return pl.pallas_call(
        functools.partial(_post_kernel, final),
        grid=(n // tile,),
        in_specs=[_rows(D_MODEL, tile), _rows(a.shape[1], tile), _resident(wout.shape), _resident(gffn.shape),
                  _resident(wgu.shape), _resident(wd.shape), _resident(gple.shape), _resident(wpg.shape),
                  _rows(PLE_DIM, tile), _resident(wpp.shape), _resident(gfin.shape)],
        out_specs=[_rows(D_MODEL, tile)] * n_out,
        out_shape=[jax.ShapeDtypeStruct((n, D_MODEL), F32)] * n_out,
        compiler_params=_params(1),
        name="post",
    )(h, a, wout, gffn, wgu, wd, gple, wpg, ple, wpp, gfin)


def _cmp_kernel(rows_ref, pe_ref, phik_ref, phiv_ref, kc_ref, vc_ref):
    t = rows_ref.shape[1]
    nsb = t // SEL_BLOCK
    x3 = rows_ref[0, :, 0:256].reshape(nsb, SEL_BLOCK, 256)
    pe = pe_ref[...][None]
    even = jnp.mean(x3[:, 0:CMP_BLOCK, :] + pe, axis=1)
    odd = jnp.mean(x3[:, CMP_BLOCK:SEL_BLOCK, :] + pe, axis=1)
    km = jnp.concatenate([even, odd], axis=0)
    kc_ref[0] = _dot(km[:, 0:128], phik_ref[...])
    vc_ref[0] = _dot(km[:, 128:256], phiv_ref[...])


def _cmp_prompt(rows3, pe, phik, phiv):
    b, t, _ = rows3.shape
    ncb = t // CMP_BLOCK
    return pl.pallas_call(
        _cmp_kernel,
        grid=(b,),
        in_specs=[pl.BlockSpec((1, t, 512), lambda i: (i, 0, 0)), _resident(pe.shape),
                  _resident(phik.shape), _resident(phiv.shape)],
        out_specs=[pl.BlockSpec((1, ncb, LANES), lambda i: (i, 0, 0))] * 2,
        out_shape=[jax.ShapeDtypeStruct((b, ncb, LANES), F32)] * 2,
        compiler_params=_params(1),
        name="nsa_cmp",
    )(rows3, pe, phik, phiv)


def _nsa_prompt_kernel(qc_ref, qr_ref, rows_ref, win_ref, kc_ref, vc_ref, misc_ref, exp_ref, o_ref):
    qi = pl.program_id(1)
    tq = qc_ref.shape[1]
    t_len = rows_ref.shape[1]
    nsb = t_len // SEL_BLOCK
    ncb = 2 * nsb
    q0 = qi * tq
    gates_t = jax.nn.sigmoid(misc_ref[0].T)
    kc = kc_ref[0].astype(BF16)
    vc_t = vc_ref[0].T.astype(BF16)
    qpos1 = q0 + lax.broadcasted_iota(jnp.int32, (1, tq), 1)
    qpos4 = _tile_lanes(qpos1, NSA_GROUP)
    out_rows = []
    for g in range(NSA_KV_HEADS):
        heads = range(g * NSA_GROUP, (g + 1) * NSA_GROUP)
        qc_g = jnp.concatenate([qc_ref[0, :, h * LANES:(h + 1) * LANES] for h in heads], axis=0).astype(BF16)
        qr_g = jnp.concatenate([qr_ref[0, :, h * LANES:(h + 1) * LANES] for h in heads], axis=0).astype(BF16)
        s_c = _dot_nt(kc, qc_g)
        n_idx = lax.broadcasted_iota(jnp.int32, (ncb, NSA_GROUP * tq), 0)
        n_orig = jnp.where(n_idx < nsb, 2 * n_idx, 2 * (n_idx - nsb) + 1)
        cmask = (n_orig + 1) * CMP_BLOCK - 1 <= qpos4
        (p_c,), _, _ = _softmax_cols([(s_c, cmask)])
        o_c = jnp.dot(vc_t, p_c.astype(BF16), preferred_element_type=F32)[g * HEAD_DIM:(g + 1) * HEAD_DIM]
        imp = sum(p_c[:, r * tq:(r + 1) * tq] for r in range(NSA_GROUP))
        imp = imp[0:nsb] + imp[nsb:ncb]
        blk = lax.broadcasted_iota(jnp.int32, (nsb, tq), 0)
        forced = (blk == 0) | (blk == (qpos1 >> 6))
        val = jnp.where(forced, jnp.inf, jnp.where(blk * SEL_BLOCK <= qpos1, imp, -jnp.inf))
        rank = jnp.zeros((nsb, tq), F32)
        for i in range(nsb):
            vi = val[i:i + 1, :]
            rank = rank + jnp.where(vi > val, 1.0, 0.0) + jnp.where(vi == val, jnp.where(blk > i, 1.0, 0.0), 0.0)
        sel = jnp.where(rank < min(N_SEL, nsb), jnp.where(val > -jnp.inf, 1.0, 0.0), 0.0).astype(BF16)

        init = (jnp.full((1, NSA_GROUP * tq), NEG, F32), jnp.zeros((1, NSA_GROUP * tq), F32),
                jnp.zeros((HEAD_DIM, NSA_GROUP * tq), F32))

        def slc_body(kt, carry):
            k0 = pl.multiple_of(kt * NSA_TK, NSA_TK)
            k = rows_ref[0, pl.ds(k0, NSA_TK), 256:384]
            v = rows_ref[0, pl.ds(k0, NSA_TK), 384:512]
            s_t = _dot_nt(k, qr_g)
            chosen = jnp.dot(exp_ref[pl.ds(k0, NSA_TK), :], sel, preferred_element_type=F32)
            kpos = k0 + lax.broadcasted_iota(jnp.int32, (NSA_TK, tq), 0)
            mask = _tile_lanes((chosen > 0.5) & (kpos <= qpos1), NSA_GROUP)
            v_t = v.T[g * HEAD_DIM:(g + 1) * HEAD_DIM].astype(BF16)
            return _flash_update(carry, s_t, mask, v_t)

        n_kt = (q0 + tq - 1) // NSA_TK + 1
        _, l_s, acc_s = lax.fori_loop(0, n_kt, slc_body, init)
        o_s = acc_s / l_s

        carry = init
        n_back_tiles = -(-(NSA_WINDOW - 1) // tq)
        for j in range(n_back_tiles + 1):
            kt = qi - n_back_tiles + j
            k0 = pl.multiple_of(jnp.maximum(kt, 0) * tq, tq)
            k = win_ref[0, pl.ds(k0, tq), 0:128]
            v = win_ref[0, pl.ds(k0, tq), 128:256]
            s_t = _dot_nt(k, qr_g)
            kpos = kt * tq + lax.broadcasted_iota(jnp.int32, (tq, tq), 0)
            mask = _tile_lanes((kpos <= qpos1) & (kpos >= qpos1 - (NSA_WINDOW - 1)) & (kpos >= 0), NSA_GROUP)
            v_t = v.T[g * HEAD_DIM:(g + 1) * HEAD_DIM].astype(BF16)
            carry = _flash_update(carry, s_t, mask, v_t)
        o_w = carry[2] / carry[1]

        for r, h in enumerate(heads):
            sl = slice(r * tq, (r + 1) * tq)
            row = 32 + h
            out_rows.append(gates_t[row:row + 1] * o_c[:, sl]
                            + gates_t[row + 8:row + 9] * o_s[:, sl]
                            + gates_t[row + 16:row + 17] * o_w[:, sl])
    o_ref[0] = jnp.concatenate(out_rows, axis=0).T


def _nsa_prompt(qc3, qr3, rows3, win3, kc, vc, misc3, expand):
    b, t, _ = rows3.shape
    tq = min(NSA_TQ, t)
    ncb = kc.shape[1]
    qspec = pl.BlockSpec((1, tq, NSA_HEADS * LANES), lambda i, j: (i, j, 0))
    return pl.pallas_call(
        _nsa_prompt_kernel,
        grid=(b, t // tq),
        in_specs=[qspec, qspec,
                  pl.BlockSpec((1, t, 512), lambda i, j: (i, 0, 0)),
                  pl.BlockSpec((1, t, 256), lambda i, j: (i, 0, 0)),
                  pl.BlockSpec((1, ncb, LANES), lambda i, j: (i, 0, 0)),
                  pl.BlockSpec((1, ncb, LANES), lambda i, j: (i, 0, 0)),
                  pl.BlockSpec((1, tq, LANES), lambda i, j: (i, j, 0)),
                  _resident(expand.shape)],
        out_specs=pl.BlockSpec((1, tq, NSA_HEADS * HEAD_DIM), lambda i, j: (i, j, 0)),
        out_shape=jax.ShapeDtypeStruct((b, t, NSA_HEADS * HEAD_DIM), F32),
        compiler_params=_params(2),
        name="nsa_prompt",
    )(qc3, qr3, rows3, win3, kc, vc, misc3, expand)


def _mla_prompt_kernel(q_ref, k_ref, v_ref, o_ref):
    qi = pl.program_id(2)
    tq = q_ref.shape[1]
    tk = min(MLA_TK, k_ref.shape[1])
    q0 = qi * tq
    qpos = q0 + lax.broadcasted_iota(jnp.int32, (1, tq), 1)
    scale = (QK_NOPE + QK_ROPE) ** -0.5
    outs = []
    for hh in range(2):
        q = q_ref[0, :, hh * LANES:(hh + 1) * LANES].astype(BF16)

        def body(kt, carry):
            k0 = pl.multiple_of(kt * tk, tk)
            k = k_ref[0, pl.ds(k0, tk), hh * LANES:(hh + 1) * LANES]
            s_t = _dot_nt(k, q) * scale
            kpos = k0 + lax.broadcasted_iota(jnp.int32, (tk, tq), 0)
            v_t = v_ref[0, pl.ds(k0, tk), :].T[hh * V_DIM:(hh + 1) * V_DIM].astype(BF16)
            return _flash_update(carry, s_t, kpos <= qpos, v_t)

        init = (jnp.full((1, tq), NEG, F32), jnp.zeros((1, tq), F32), jnp.zeros((V_DIM, tq), F32))
        _, l, acc = lax.fori_loop(0, (q0 + tq - 1) // tk + 1, body, init)
        outs.append(acc / l)
    o_ref[0] = jnp.concatenate(outs, axis=0).T


def _mla_prompt(qf3, kf3, v3):
    b, t, _ = qf3.shape
    tq = min(MLA_TQ, t)
    return pl.pallas_call(
        _mla_prompt_kernel,
        grid=(b, MLA_HEADS // 2, t // tq),
        in_specs=[pl.BlockSpec((1, tq, 2 * LANES), lambda i, p, j: (i, j, p)),
                  pl.BlockSpec((1, t, 2 * LANES), lambda i, p, j: (i, 0, p)),
                  pl.BlockSpec((1, t, 2 * V_DIM), lambda i, p, j: (i, 0, p))],
        out_specs=pl.BlockSpec((1, tq, 2 * V_DIM), lambda i, p, j: (i, j, p)),
        out_shape=jax.ShapeDtypeStruct((b, t, MLA_HEADS * V_DIM), F32),
        compiler_params=_params(3),
        name="mla_prompt",
    )(qf3, kf3, v3)


def _dil_prompt_kernel(sub, has_prev, q_ref, kc_ref, vc_ref, kp_ref, vp_ref, o_ref, lse_ref):
    i = pl.program_id(2)
    tq = q_ref.shape[1]
    lane = lax.broadcasted_iota(jnp.int32, (1, LANES), 1)
    half_masks = [jnp.where(lane < HEAD_DIM, 1.0, 0.0), jnp.where(lane >= HEAD_DIM, 1.0, 0.0)]
    n_keys = 2 * sub if has_prev else sub
    c_idx = lax.broadcasted_iota(jnp.int32, (n_keys, sub), 0)
    q_idx = lax.broadcasted_iota(jnp.int32, (n_keys, sub), 1)
    kpos = c_idx - sub if has_prev else c_idx
    band = (kpos <= q_idx) & (kpos >= q_idx - DIL_BACK)
    for s in range(tq // sub):
        rs = slice(s * sub, (s + 1) * sub)
        q = q_ref[0, rs, :]
        k = kc_ref[0, rs, :]
        v = vc_ref[0, rs, :]
        mask = band
        if has_prev:
            if s == 0:
                k = jnp.concatenate([kp_ref[0], k], axis=0)
                v = jnp.concatenate([vp_ref[0], v], axis=0)
                mask = band & (c_idx >= jnp.where(i > 0, 0, sub))
            else:
                ps = slice((s - 1) * sub, s * sub)
                k = jnp.concatenate([kc_ref[0, ps, :], k], axis=0)
                v = jnp.concatenate([vc_ref[0, ps, :], v], axis=0)
        o_rows, lse_rows = [], []
        for j in range(DIL_HEADS // 2):
            cs = slice(j * LANES, (j + 1) * LANES)
            v_t = v[:, cs].T.astype(BF16)
            k_j = k[:, cs].astype(BF16)
            for hh in range(2):
                s_t = _dot_nt(k_j, q[:, cs] * half_masks[hh])
                (p,), den, m = _softmax_cols([(s_t, mask)])
                o_rows.append(jnp.dot(v_t[hh * HEAD_DIM:(hh + 1) * HEAD_DIM], p.astype(BF16),
                                      preferred_element_type=F32))
                lse_rows.append(jnp.broadcast_to(jnp.log(den) + m, (HEAD_DIM, sub)))
        o_ref[0, rs, :] = jnp.concatenate(o_rows, axis=0).T
        lse_ref[0, rs, :] = jnp.concatenate(lse_rows, axis=0).T


def _dil_prompt(q, kv, b, t, dil):
    hw = DIL_HEADS * HEAD_DIM
    l = t // dil
    sub = min(LANES, l)
    has_prev = l > sub
    tq = min(l, 4 * sub)
    per = tq // sub
    q3 = q.reshape(b, l, dil * hw)
    kv3 = kv.reshape(b, l, dil * 2 * hw)
    cur = lambda off: pl.BlockSpec((1, tq, hw), lambda bi, r, i: (bi, i, 2 * r + off))
    prev = lambda off: pl.BlockSpec((1, sub, hw), lambda bi, r, i: (bi, jnp.maximum(i * per - 1, 0), 2 * r + off))
    qo = pl.BlockSpec((1, tq, hw), lambda bi, r, i: (bi, i, r))
    o, lse = pl.pallas_call(
        functools.partial(_dil_prompt_kernel, sub, has_prev),
        grid=(b, dil, l // tq),
        in_specs=[qo, cur(0), cur(1), prev(0), prev(1)],
        out_specs=[qo, qo],
        out_shape=[jax.ShapeDtypeStruct((b, l, dil * hw), F32)] * 2,
        compiler_params=_params(3),
        name="dil_prompt",
    )(q3, kv3, kv3, kv3, kv3)
    return o.reshape(b * t, hw), lse.reshape(b * t, hw)


def _dil_merge_kernel(o0, o1, o2, l0, l1, l2, out_ref):
    ls = [l0[...], l1[...], l2[...]]
    mx = jnp.maximum(jnp.maximum(ls[0], ls[1]), ls[2])
    ws = [jnp.exp(x - mx) for x in ls]
    num = ws[0] * o0[...] + ws[1] * o1[...] + ws[2] * o2[...]
    out_ref[...] = num / (ws[0] + ws[1] + ws[2])


def _dil_merge(outs, lses, tile):
    n, hw = outs[0].shape
    return pl.pallas_call(
        _dil_merge_kernel,
        grid=(n // tile,),
        in_specs=[_rows(hw, tile)] * 6,
        out_specs=_rows(hw, tile),
        out_shape=jax.ShapeDtypeStruct((n, hw), F32),
        compiler_params=_params(1),
        name="dil_merge",
    )(*outs, *lses)


def _snsa_a_kernel(layer, past_len, t_new, pt_ref, *refs):
    pages = refs[:PAGES_PER_STEP]
    qc_ref, pe_ref, phik_ref, phiv_ref, ocmp_ref, idx_ref, cme_ref, cmo_ref = refs[PAGES_PER_STEP:]
    del layer
    step = pl.program_id(1)
    n_steps = pl.num_programs(1)
    pe = pe_ref[...][None]
    for quad in range(PAGES_PER_STEP // 4):
        x = jnp.concatenate([pages[4 * quad + j][0] for j in range(4)], axis=0)
        x3 = x.reshape(8, SEL_BLOCK, 256)
        r0 = pl.multiple_of(step * (2 * PAGES_PER_STEP) + quad * 8, 8)
        cme_ref[pl.ds(r0, 8), :] = jnp.mean(x3[:, 0:CMP_BLOCK, :] + pe, axis=1)
        cmo_ref[pl.ds(r0, 8), :] = jnp.mean(x3[:, CMP_BLOCK:SEL_BLOCK, :] + pe, axis=1)

    @pl.when(step == n_steps - 1)
    def _():
        nsb_past = cme_ref.shape[0]
        ncb = 2 * nsb_past
        km = jnp.concatenate([cme_ref[...], cmo_ref[...]], axis=0)
        kc = _dot(km[:, 0:128], phik_ref[...]).astype(BF16)
        vc = _dot(km[:, 128:256], phiv_ref[...]).astype(BF16)
        rows = NSA_GROUP * 8
        t_row = lax.broadcasted_iota(jnp.int32, (rows, 1), 0) & 7
        pos = past_len + t_row
        n_idx = lax.broadcasted_iota(jnp.int32, (rows, ncb), 1)
        n_orig = jnp.where(n_idx < nsb_past, 2 * n_idx, 2 * (n_idx - nsb_past) + 1)
        cmask = (n_orig + 1) * CMP_BLOCK - 1 <= pos
        nsb = -(-(past_len + t_new) // SEL_BLOCK)
        wide = -(-nsb // LANES) * LANES
        blk = lax.broadcasted_iota(jnp.int32, (8, wide), 1)
        blk_f = blk.astype(F32)
        pos8 = past_len + lax.broadcasted_iota(jnp.int32, (8, 1), 0)
        lane = lax.broadcasted_iota(jnp.int32, (8, LANES), 1)
        for g in range(NSA_KV_HEADS):
            s = _dot_nt(qc_ref[0, g], kc)
            (p,), _, _ = _softmax_rows([(s, cmask)])
            ocmp_ref[0, g] = jnp.dot(p.astype(BF16), vc, preferred_element_type=F32)
            imp = sum(p[r * 8:(r + 1) * 8, 0:nsb_past] + p[r * 8:(r + 1) * 8, nsb_past:ncb]
                      for r in range(NSA_GROUP))
            imp = jnp.concatenate([imp, jnp.zeros((8, wide - nsb_past), F32)], axis=1)
            forced = (blk == 0) | (blk == (pos8 >> 6))
            val = jnp.where(forced, jnp.inf, jnp.where(blk * SEL_BLOCK <= pos8, imp, -jnp.inf))
            picks = jnp.full((8, LANES), -1.0, F32)
            for k in range(min(N_SEL, nsb)):
                mx = jnp.max(val, axis=1, keepdims=True)
                first = jnp.min(jnp.where(val == mx, blk_f, 1e9), axis=1, keepdims=True)
                picks = jnp.where(lane == k, jnp.where(mx > -jnp.inf, first, -1.0), picks)
                val = jnp.where(blk_f == first, -jnp.inf, val)
            idx_ref[0, g] = picks.astype(jnp.int32)


def _snsa_a(layer, pt, cache_v, qc_st, pe, phik, phiv, t_new):
    b, n_pages = pt.shape
    past_len = n_pages * LANES
    nsb_past = past_len // SEL_BLOCK
    page = lambda j: pl.BlockSpec(
        (1, LANES, 256), lambda bi, s, pt_ref: (pt_ref[bi, s * PAGES_PER_STEP + j], 0, 2 * layer))
    const = lambda shape: pl.BlockSpec(shape, lambda bi, s, pt_ref: (0,) * len(shape))
    grid_spec = pltpu.PrefetchScalarGridSpec(
        num_scalar_prefetch=1,
        grid=(b, n_pages // PAGES_PER_STEP),
        in_specs=[page(j) for j in range(PAGES_PER_STEP)] + [
            pl.BlockSpec((1, NSA_KV_HEADS, NSA_GROUP * 8, LANES), lambda bi, s, pt_ref: (bi, 0, 0, 0)),
            const(pe.shape), const(phik.shape), const(phiv.shape)],
        out_specs=[pl.BlockSpec((1, NSA_KV_HEADS, NSA_GROUP * 8, LANES), lambda bi, s, pt_ref: (bi, 0, 0, 0)),
                   pl.BlockSpec((1, NSA_KV_HEADS, 8, LANES), lambda bi, s, pt_ref: (bi, 0, 0, 0))],
        scratch_shapes=[pltpu.VMEM((nsb_past, 256), F32), pltpu.VMEM((nsb_past, 256), F32)],
    )
    return pl.pallas_call(
        functools.partial(_snsa_a_kernel, layer, past_len, t_new),
        grid_spec=grid_spec,
        out_shape=[jax.ShapeDtypeStruct((b, NSA_KV_HEADS, NSA_GROUP * 8, LANES), F32),
                   jax.ShapeDtypeStruct((b, NSA_KV_HEADS, 8, LANES), jnp.int32)],
        compiler_params=_params(2),
        name="snsa_cmp_topk",
    )(pt, *([cache_v] * PAGES_PER_STEP), qc_st, pe, phik, phiv)


def _snsa_b_kernel(past_len, t_new, n_t, pt_ref, sidx_ref, *refs):
    n_sel = N_SEL
    blocks = refs[:n_sel]
    q_ref, new_ref, o_ref = refs[n_sel:]
    del pt_ref
    bi, ti, gi = pl.program_id(0), pl.program_id(1), pl.program_id(2)
    base = ((bi * n_t + ti) * NSA_KV_HEADS + gi) * n_sel
    new_blk = past_len // SEL_BLOCK
    pos = past_len + ti
    q = q_ref[0, 0, 0].astype(BF16)
    k_new = new_ref[0, :, 0:128]
    v_new = new_ref[0, :, 128:256]
    offs = lax.broadcasted_iota(jnp.int32, (8, 2 * SEL_BLOCK), 1) & (SEL_BLOCK - 1)
    first_half = lax.broadcasted_iota(jnp.int32, (8, 2 * SEL_BLOCK), 1) < SEL_BLOCK
    pieces, vals = [], []
    for jj in range(n_sel // 2):
        ks, vs, idxs = [], [], []
        for j in (2 * jj, 2 * jj + 1):
            idx = sidx_ref[base + j]
            is_new = idx == new_blk
            blk = blocks[j][0, 0]
            ks.append(jnp.where(is_new, k_new, blk[:, 0:128]))
            vs.append(jnp.where(is_new, v_new, blk[:, 128:256]))
            idxs.append(idx)
        k2 = jnp.concatenate(ks, axis=0)
        vals.append(jnp.concatenate(vs, axis=0).astype(BF16))
        s = _dot_nt(q, k2)
        idx2 = jnp.where(first_half, idxs[0], idxs[1])
        kpos = idx2 * SEL_BLOCK + offs
        pieces.append((s, (idx2 >= 0) & (kpos <= pos) & (kpos < past_len + t_new)))
    ps, _, _ = _softmax_rows(pieces)
    o_ref[0, 0, 0] = sum(jnp.dot(p.astype(BF16), v, preferred_element_type=F32) for p, v in zip(ps, vals))


def _snsa_b(layer, pt, sidx, cache_blk, q_tg, new_blk_rows, t_new):
    b, n_pages = pt.shape
    past_len = n_pages * LANES
    n_cache_blk = past_len // SEL_BLOCK

    def blk_spec(j):
        def imap(bi, ti, gi, pt_ref, sidx_ref):
            idx = sidx_ref[((bi * t_new + ti) * NSA_KV_HEADS + gi) * N_SEL + j]
            idx = jnp.clip(idx, 0, n_cache_blk - 1)
            return (pt_ref[bi, idx // 2], idx % 2, 0, 2 * layer + 1)
        return pl.BlockSpec((1, 1, SEL_BLOCK, 256), imap)

    grid_spec = pltpu.PrefetchScalarGridSpec(
        num_scalar_prefetch=2,
        grid=(b, t_new, NSA_KV_HEADS),
        in_specs=[blk_spec(j) for j in range(N_SEL)] + [
            pl.BlockSpec((1, 1, 1, 8, LANES), lambda bi, ti, gi, *_: (bi, ti, gi, 0, 0)),
            pl.BlockSpec((1, SEL_BLOCK, 256), lambda bi, ti, gi, *_: (bi, 0, 0))],
        out_specs=pl.BlockSpec((1, 1, 1, 8, LANES), lambda bi, ti, gi, *_: (bi, ti, gi, 0, 0)),
    )
    return pl.pallas_call(
        functools.partial(_snsa_b_kernel, past_len, t_new, t_new),
        grid_spec=grid_spec,
        out_shape=jax.ShapeDtypeStruct((b, t_new, NSA_KV_HEADS, 8, LANES), F32),
        compiler_params=_params(3),
        name="snsa_slc",
    )(pt, sidx, *([cache_blk] * N_SEL), q_tg, new_blk_rows)


def _snsa_win_kernel(past_len, t_new, q_ref, c_ref, new_ref, o_ref):
    lw = c_ref.shape[2]
    rows = NSA_GROUP * 8
    t_row = lax.broadcasted_iota(jnp.int32, (rows, 1), 0) & 7
    pos = past_len + t_row
    kpos_c = (past_len - lw) + lax.broadcasted_iota(jnp.int32, (rows, lw), 1)
    c_new = lax.broadcasted_iota(jnp.int32, (rows, LANES), 1)
    kpos_n = past_len + c_new
    mask_c = (kpos_c <= pos) & (kpos_c >= pos - (NSA_WINDOW - 1))
    mask_n = (c_new < t_new) & (kpos_n <= pos) & (kpos_n >= pos - (NSA_WINDOW - 1))
    kc, vc = c_ref[0, 0, :, 0:128], c_ref[0, 0, :, 128:256]
    kn, vn = new_ref[0, :, 0:128], new_ref[0, :, 128:256]
    for g in range(NSA_KV_HEADS):
        q = q_ref[0, g]
        (p1, p2), _, _ = _softmax_rows([(_dot_nt(q, kc), mask_c), (_dot_nt(q, kn), mask_n)])
        o_ref[0, g] = _dot(p1, vc) + _dot(p2, vn)


def _snsa_win(layer, qr_st, cache_win, win_new, past_len, t_new):
    b = qr_st.shape[0]
    lw = cache_win.shape[2]
    return pl.pallas_call(
        functools.partial(_snsa_win_kernel, past_len, t_new),
        grid=(b,),
        in_specs=[pl.BlockSpec((1, NSA_KV_HEADS, NSA_GROUP * 8, LANES), lambda i: (i, 0, 0, 0)),
                  pl.BlockSpec((1, 1, lw, 256), lambda i: (layer, i, 0, 0)),
                  pl.BlockSpec((1, LANES, 256), lambda i: (i, 0, 0))],
        out_specs=pl.BlockSpec((1, NSA_KV_HEADS, NSA_GROUP * 8, LANES), lambda i: (i, 0, 0, 0)),
        out_shape=jax.ShapeDtypeStruct((b, NSA_KV_HEADS, NSA_GROUP * 8, LANES), F32),
        compiler_params=_params(1),
        name="snsa_win",
    )(qr_st, cache_win, win_new)


def _scomb_kernel(t_new, oc_ref, os_ref, ow_ref, misc_ref, o_ref):
    gates_t = jax.nn.sigmoid(misc_ref[0].T)
    pad = jnp.zeros((8 - t_new, LANES), F32)
    rows = []
    for g in range(NSA_KV_HEADS):
        gs = slice(g * HEAD_DIM, (g + 1) * HEAD_DIM)
        for r in range(NSA_GROUP):
            h = g * NSA_GROUP + r
            oc = oc_ref[0, g, r * 8:(r + 1) * 8, :].T[gs]
            ow = ow_ref[0, g, r * 8:(r + 1) * 8, :].T[gs]
            os_ = jnp.concatenate([os_ref[0, t, g, r:r + 1, :] for t in range(t_new)] + [pad], axis=0).T[gs]
            row = 32 + h
            rows.append(gates_t[row:row + 1] * oc + gates_t[row + 8:row + 9] * os_ + gates_t[row + 16:row + 17] * ow)
    o_ref[0] = jnp.concatenate(rows, axis=0).T


def _scomb(oc, os_, ow, misc_pad, t_new):
    b = oc.shape[0]
    grp = pl.BlockSpec((1, NSA_KV_HEADS, NSA_GROUP * 8, LANES), lambda i: (i, 0, 0, 0))
    return pl.pallas_call(
        functools.partial(_scomb_kernel, t_new),
        grid=(b,),
        in_specs=[grp, pl.BlockSpec((1, t_new, NSA_KV_HEADS, 8, LANES), lambda i: (i, 0, 0, 0, 0)), grp,
                  pl.BlockSpec((1, 8, LANES), lambda i: (i, 0, 0))],
        out_specs=pl.BlockSpec((1, 8, NSA_HEADS * HEAD_DIM), lambda i: (i, 0, 0)),
        out_shape=jax.ShapeDtypeStruct((b, 8, NSA_HEADS * HEAD_DIM), F32),
        compiler_params=_params(1),
        name="snsa_combine",
    )(oc, os_, ow, misc_pad)


def _smla_kernel(layer, t_new, pt_ref, *refs):
    pages = refs[:PAGES_PER_STEP]
    (qf_ref, wuka_ref, wukb_ref, wuva_ref, wuvb_ref, new_ref, o_ref,
     qa_ref, qb_ref, m_ref, l_ref, acca_ref, accb_ref) = refs[PAGES_PER_STEP:]
    del pt_ref, layer
    step = pl.program_id(1)
    n_steps = pl.num_programs(1)
    scale = (QK_NOPE + QK_ROPE) ** -0.5
    rows = MLA_HEADS * 8

    @pl.when(step == 0)
    def _():
        for h in range(MLA_HEADS):
            qh = qf_ref[0, :, h * LANES:(h + 1) * LANES].astype(BF16)
            qa_ref[h * 8:(h + 1) * 8, :] = jnp.dot(qh, wuka_ref[h], preferred_element_type=F32)
            qb_ref[h * 8:(h + 1) * 8, :] = jnp.dot(qh, wukb_ref[h], preferred_element_type=F32)
        m_ref[...] = jnp.full(m_ref.shape, NEG, F32)
        l_ref[...] = jnp.zeros(l_ref.shape, F32)
        acca_ref[...] = jnp.zeros(acca_ref.shape, F32)
        accb_ref[...] = jnp.zeros(accb_ref.shape, F32)

    qa = qa_ref[...].astype(BF16)
    qb = qb_ref[...].astype(BF16)

    def absorb(xa, xb, mask):
        s = (_dot_nt(qa, xa) + _dot_nt(qb, xb)) * scale
        m_old = m_ref[...]
        m_new = jnp.maximum(m_old, jnp.max(jnp.where(mask, s, NEG), axis=1, keepdims=True))
        p = jnp.where(mask, jnp.exp(s - m_new), 0.0)
        alpha = jnp.exp(m_old - m_new)
        l_ref[...] = alpha * l_ref[...] + jnp.sum(p, axis=1, keepdims=True)
        pb = p.astype(BF16)
        acca_ref[...] = alpha * acca_ref[...] + jnp.dot(pb, xa.astype(BF16), preferred_element_type=F32)
        accb_ref[...] = alpha * accb_ref[...] + jnp.dot(pb, xb.astype(BF16), preferred_element_type=F32)
        m_ref[...] = m_new

    wa = qa_ref.shape[1]
    width = wa + qb_ref.shape[1]
    all_keys = lax.broadcasted_iota(jnp.int32, (rows, LANES), 1) >= 0
    for j in range(PAGES_PER_STEP):
        absorb(pages[j][0, :, 0:wa], pages[j][0, :, wa:width], all_keys)

    @pl.when(step == n_steps - 1)
    def _():
        c = lax.broadcasted_iota(jnp.int32, (rows, LANES), 1)
        t_row = lax.broadcasted_iota(jnp.int32, (rows, LANES), 0) & 7
        absorb(new_ref[0, :, 0:wa], new_ref[0, :, wa:width], (c < t_new) & (c <= t_row))
        inv = 1.0 / l_ref[...]
        oa = (acca_ref[...] * inv).astype(BF16)
        ob = (accb_ref[...] * inv).astype(BF16)
        outs = []
        for h in range(MLA_HEADS):
            hs = slice(h * 8, (h + 1) * 8)
            outs.append(jnp.dot(oa[hs], wuva_ref[h], preferred_element_type=F32)
                        + jnp.dot(ob[hs], wuvb_ref[h], preferred_element_type=F32))
        o_ref[0] = jnp.concatenate([jnp.concatenate(outs[2 * j:2 * j + 2], axis=1)
                                    for j in range(MLA_HEADS // 2)], axis=1)


def _smla(layer, pt, cache_v, qf_pad, wuka, wukb, wuva, wuvb, new_pad, t_new):
    b, n_pages = pt.shape
    width = cache_v.shape[2]
    page = lambda j: pl.BlockSpec(
        (1, LANES, width), lambda bi, s, pt_ref: (pt_ref[bi, s * PAGES_PER_STEP + j], 0, 0))
    const = lambda shape: pl.BlockSpec(shape, lambda bi, s, pt_ref: (0,) * len(shape))
    rows = MLA_HEADS * 8
    wa = wuka.shape[2]
    grid_spec = pltpu.PrefetchScalarGridSpec(
        num_scalar_prefetch=1,
        grid=(b, n_pages // PAGES_PER_STEP),
        in_specs=[page(j) for j in range(PAGES_PER_STEP)] + [
            pl.BlockSpec((1, 8, MLA_HEADS * LANES), lambda bi, s, pt_ref: (bi, 0, 0)),
            const(wuka.shape), const(wukb.shape), const(wuva.shape), const(wuvb.shape),
            pl.BlockSpec((1, LANES, width), lambda bi, s, pt_ref: (bi, 0, 0))],
        out_specs=pl.BlockSpec((1, 8, MLA_HEADS * V_DIM), lambda bi, s, pt_ref: (bi, 0, 0)),
        scratch_shapes=[pltpu.VMEM((rows, wa), F32), pltpu.VMEM((rows, width - wa), F32),
                        pltpu.VMEM((rows, 1), F32), pltpu.VMEM((rows, 1), F32),
                        pltpu.VMEM((rows, wa), F32), pltpu.VMEM((rows, width - wa), F32)],
    )
    return pl.pallas_call(
        functools.partial(_smla_kernel, layer, t_new),
        grid_spec=grid_spec,
        out_shape=jax.ShapeDtypeStruct((b, 8, MLA_HEADS * V_DIM), F32),
        compiler_params=_params(2),
        name="smla",
    )(pt, *([cache_v] * PAGES_PER_STEP), qf_pad, wuka, wukb, wuva, wuvb, new_pad)


def _sdil_kernel(t_new, q_ref, new_ref, c0_ref, c1_ref, c2_ref, o_ref):
    hw = DIL_HEADS * HEAD_DIM
    rows = DIL_HEADS * 8
    r_idx = lax.broadcasted_iota(jnp.int32, (rows, hw), 0)
    l_idx = lax.broadcasted_iota(jnp.int32, (rows, hw), 1)
    head_mask = jnp.where((r_idx >> 3) == (l_idx >> 6), 1.0, 0.0)
    outs, lses = [], []
    for g, (c_ref, (window, dil)) in enumerate(zip((c0_ref, c1_ref, c2_ref), DIL_PATTERNS)):
        lb = c_ref.shape[2]
        q = q_ref[0, :, g * hw:(g + 1) * hw]
        qs = jnp.concatenate([q] * DIL_HEADS, axis=0) * head_mask
        kc, vc = c_ref[0, 0, :, 0:hw], c_ref[0, 0, :, hw:2 * hw]
        kn = new_ref[0, :, g * 2 * hw:g * 2 * hw + hw]
        vn = new_ref[0, :, g * 2 * hw + hw:(g + 1) * 2 * hw]
        t_c = lax.broadcasted_iota(jnp.int32, (rows, lb), 0) & 7
        d_c = lb + t_c - lax.broadcasted_iota(jnp.int32, (rows, lb), 1)
        mask_c = (d_c >= 0) & ((d_c & (dil - 1)) == 0) & (d_c <= window)
        t_n = lax.broadcasted_iota(jnp.int32, (rows, LANES), 0) & 7
        c_n = lax.broadcasted_iota(jnp.int32, (rows, LANES), 1)
        d_n = t_n - c_n
        mask_n = (c_n < t_new) & (d_n >= 0) & ((d_n & (dil - 1)) == 0) & (d_n <= window)
        (p1, p2), den, m = _softmax_rows([(_dot_nt(qs, kc), mask_c), (_dot_nt(qs, kn), mask_n)])
        o = (_dot(p1, vc) + _dot(p2, vn)) * head_mask
        outs.append(jnp.sum(o.reshape(DIL_HEADS, 8, hw), axis=0))
        lse = (jnp.log(den) + m) * head_mask
        lses.append(jnp.sum(lse.reshape(DIL_HEADS, 8, hw), axis=0))
    mx = jnp.maximum(jnp.maximum(lses[0], lses[1]), lses[2])
    ws = [jnp.exp(x - mx) for x in lses]
    o_ref[0] = (ws[0] * outs[0] + ws[1] * outs[1] + ws[2] * outs[2]) / (ws[0] + ws[1] + ws[2])


def _sdil(layer, q_pad, new_pad, caches, t_new):
    b = q_pad.shape[0]
    hw = DIL_HEADS * HEAD_DIM
    cspec = lambda c: pl.BlockSpec((1, 1, c.shape[2], 2 * hw), lambda i: (layer, i, 0, 0))
    return pl.pallas_call(
        functools.partial(_sdil_kernel, t_new),
        grid=(b,),
        in_specs=[pl.BlockSpec((1, 8, N_DIL * hw), lambda i: (i, 0, 0)),
                  pl.BlockSpec((1, LANES, N_DIL * 2 * hw), lambda i: (i, 0, 0))] + [cspec(c) for c in caches],
        out_specs=pl.BlockSpec((1, 8, hw), lambda i: (i, 0, 0)),
        out_shape=jax.ShapeDtypeStruct((b, 8, hw), F32),
        compiler_params=_params(1),
        name="sdil",
    )(q_pad, new_pad, *caches)


def _pack_w_ab(w):
    q_a, kv_a, g_a, q_lat, kv_lat, kpe = jnp.split(w, np.cumsum(AB_SPLITS)[:-1].tolist(), axis=1)
    d = w.shape[0]
    qh = q_a.reshape(d, NSA_HEADS, HEAD_DIM)
    zero = jnp.zeros((d, HEAD_DIM), w.dtype)
    chunks = []
    for h in range(NSA_HEADS):
        pair = [qh[:, h], zero] if h // NSA_GROUP == 0 else [zero, qh[:, h]]
        chunks.append(jnp.concatenate(pair, axis=1))
    misc = jnp.concatenate([kpe, g_a, jnp.zeros((d, LANES - QK_ROPE - 3 * NSA_HEADS), w.dtype)], axis=1)
    return jnp.concatenate(chunks + [kv_a, q_lat, kv_lat, misc], axis=1).astype(BF16)


def _pack_w_uq(w):
    wh = w.reshape(Q_LORA, MLA_HEADS, QK_NOPE + QK_ROPE)
    pad = jnp.zeros((Q_LORA, MLA_HEADS, LANES - QK_NOPE - QK_ROPE), w.dtype)
    return jnp.concatenate([wh[..., QK_NOPE:], pad, wh[..., :QK_NOPE]], axis=-1).reshape(
        Q_LORA, MLA_HEADS * LANES).astype(BF16)


def _pack_w_kvup(w_uk, w_uv):
    k_top = jnp.concatenate([jnp.zeros((KV_LORA, MLA_HEADS, LANES - QK_NOPE), w_uk.dtype), w_uk], axis=-1)
    eye = jnp.eye(LANES, dtype=w_uk.dtype) * (np.arange(LANES) < QK_ROPE)[:, None]
    k_bot = jnp.broadcast_to(eye[:, None, :], (LANES, MLA_HEADS, LANES))
    k_part = jnp.concatenate([k_top, k_bot], axis=0).reshape(KV_LORA + LANES, MLA_HEADS * LANES)
    v_part = jnp.concatenate([w_uv.reshape(KV_LORA, MLA_HEADS * V_DIM),
                              jnp.zeros((LANES, MLA_HEADS * V_DIM), w_uv.dtype)], axis=0)
    return jnp.concatenate([k_part, v_part], axis=1).astype(BF16)


def _pack_w_absorb(w_uk, w_uv, layer, n_layers):
    width = n_layers * MLA_ROW
    off = layer * MLA_ROW
    up = jnp.zeros((MLA_HEADS, LANES, width), F32)
    up = up.at[:, LANES - QK_NOPE:, off:off + KV_LORA].set(jnp.transpose(w_uk, (1, 2, 0)))
    up = up.at[:, :QK_ROPE, off + KV_LORA:off + MLA_ROW].set(jnp.eye(QK_ROPE, dtype=F32)[None])
    down = jnp.zeros((MLA_HEADS, width, V_DIM), F32)
    down = down.at[:, off:off + KV_LORA, :].set(jnp.transpose(w_uv, (1, 0, 2)))
    up, down = up.astype(BF16), down.astype(BF16)
    wa = ((width - 1) // LANES) * LANES
    return up[:, :, :wa], up[:, :, wa:], down[:, :wa], down[:, wa:]


def _block_diag2(phi):
    z = jnp.zeros((HEAD_DIM, HEAD_DIM), phi.dtype)
    return jnp.concatenate([jnp.concatenate([phi[0], z], axis=1),
                            jnp.concatenate([z, phi[1]], axis=1)], axis=0).astype(BF16)


def _pad_rows(x, rows):
    return jnp.pad(x, ((0, 0), (0, rows - x.shape[1]), (0, 0)))


def _stack_group_rows(q, b, t):
    q5 = q.reshape(b, t, NSA_KV_HEADS, NSA_GROUP, LANES)
    q5 = jnp.pad(jnp.transpose(q5, (0, 2, 3, 1, 4)), ((0, 0), (0, 0), (0, 0), (0, 8 - t), (0, 0)))
    return q5.reshape(b, NSA_KV_HEADS, NSA_GROUP * 8, LANES)


def kernel(x_prompt, x_sample, cache_nsa_kv, cache_mla, cache_nsa_win, cache_dil0, cache_dil1, cache_dil2,
           page_table, p_prompt, p_sample, w_in_ab, w_out_ab, nsa_pe_k, nsa_pe_v, nsa_phi_k, nsa_phi_v,
           mla_q_norm, mla_kv_norm, mla_w_uq, mla_w_uk, mla_w_uv, w_in_c, w_out_c, norm_mix, norm_ffn,
           w_gate_up, w_down, norm_ple, w_ple_gate, w_ple_proj, norm_final):
    bp, tp, _ = x_prompt.shape
    bs, ts, _ = x_sample.shape
    depth = norm_mix.shape[0]
    n_ab = w_in_ab.shape[0]
    n_pool, page_size = cache_nsa_kv.shape[:2]
    n_pages = page_table.shape[1]
    past_len = n_pages * page_size
    assert page_size == LANES and tp % (2 * SEL_BLOCK) == 0 and ts <= 8 and past_len % SEL_BLOCK == 0
    assert n_pages % PAGES_PER_STEP == 0 and PAGES_PER_STEP % 4 == 0
    np_, ns_ = bp * tp, bs * ts
    hw = DIL_HEADS * HEAD_DIM

    pos_p = jnp.arange(tp, dtype=jnp.int32)
    pos_s = past_len + jnp.arange(ns_, dtype=jnp.int32) % ts
    tile_p = min(ROW_TILE, tp)
    tile_s = ns_
    tabs = {
        "p": (_rope_tables(pos_p, ROT_DIM, HEAD_DIM, 0), _rope_tables(pos_p, QK_ROPE, LANES, 0), tile_p),
        "s": (_rope_tables(pos_s, ROT_DIM, HEAD_DIM, 0), _rope_tables(pos_s, QK_ROPE, LANES, 0), tile_s),
    }
    expand = jnp.asarray(np.arange(tp)[:, None] // SEL_BLOCK == np.arange(tp // SEL_BLOCK)[None], BF16)

    cache_nsa_pages = cache_nsa_kv.reshape(n_pool, page_size, n_ab * 512)
    cache_nsa_blocks = cache_nsa_kv.reshape(n_pool, page_size // SEL_BLOCK, SEL_BLOCK, n_ab * 512)
    cache_mla_pages = cache_mla.reshape(n_pool, page_size, n_ab * MLA_ROW)
    cache_win = cache_nsa_win.reshape(cache_nsa_win.shape[:3] + (256,))
    caches_dil = [c.reshape(c.shape[:3] + (2 * hw,)) for c in (cache_dil0, cache_dil1, cache_dil2)]

    row2 = lambda v: v.reshape(1, -1)
    h_p = x_prompt.reshape(np_, D_MODEL)
    h_s = x_sample.reshape(ns_, D_MODEL)
    nsa_p, nsa_s, mla_p, mla_s, win_p, win_s = [], [], [], [], [], []
    dil_p = [[] for _ in DIL_PATTERNS]
    dil_s = [[] for _ in DIL_PATTERNS]
    y_p = y_s = None

    for i in range(depth):
        l = i // 2
        final = i == depth - 1
        gm = row2(norm_mix[i])
        if i % 2 == 0:
            w_ab = _pack_w_ab(w_in_ab[l])
            w_uq = _pack_w_uq(mla_w_uq[l])
            gq, gkv = row2(mla_q_norm[l]), row2(mla_kv_norm[l])
            pe = jnp.concatenate([nsa_pe_k[l].reshape(CMP_BLOCK, 128), nsa_pe_v[l].reshape(CMP_BLOCK, 128)], axis=1)
            phik, phiv = _block_diag2(nsa_phi_k[l]), _block_diag2(nsa_phi_v[l])
            w_out = w_out_ab[l].astype(BF16)

            t64, tm_, tile = tabs["p"]
            qc, qr, rows, win, qf, mla, misc = _pre_ab(h_p, gm, w_ab, gq, gkv, w_uq, t64, tm_, tile)
            rows3 = rows.reshape(bp, tp, 512)
            kc, vc = _cmp_prompt(rows3, pe, phik, phiv)
            o_a = _nsa_prompt(qc.reshape(bp, tp, -1), qr.reshape(bp, tp, -1), rows3, win.reshape(bp, tp, 256),
                              kc, vc, misc.reshape(bp, tp, LANES), expand)
            kf, vv = _mla_kv(mla, misc, _pack_w_kvup(mla_w_uk[l], mla_w_uv[l]), tile)
            o_b = _mla_prompt(qf.reshape(bp, tp, -1), kf.reshape(bp, tp, -1), vv.reshape(bp, tp, -1))
            mix_p = jnp.concatenate([o_a.reshape(np_, -1), o_b.reshape(np_, -1)], axis=1)
            nsa_p.append(rows3)
            mla_p.append(mla.reshape(bp, tp, MLA_ROW))
            win_p.append(win.reshape(bp, tp, 256)[:, max(tp - NSA_WINDOW, 0):])

            t64, tm_, tile = tabs["s"]
            qc, qr, rows, win, qf, mla, misc = _pre_ab(h_s, gm, w_ab, gq, gkv, w_uq, t64, tm_, tile)
            o_cmp, sel_idx = _snsa_a(l, page_table, cache_nsa_pages, _stack_group_rows(qc, bs, ts),
                                     pe, phik, phiv, ts)
            sidx = jnp.transpose(sel_idx[:, :, :ts, :N_SEL], (0, 2, 1, 3)).reshape(-1)
            q_tg = jnp.pad(qr.reshape(bs, ts, NSA_KV_HEADS, NSA_GROUP, LANES),
                           ((0, 0), (0, 0), (0, 0), (0, 8 - NSA_GROUP), (0, 0)))
            new_blk = _pad_rows(rows.reshape(bs, ts, 512)[:, :, 256:512], SEL_BLOCK)
            o_slc = _snsa_b(l, page_table, sidx, cache_nsa_blocks, q_tg, new_blk, ts)
            o_win = _snsa_win(l, _stack_group_rows(qr, bs, ts), cache_win,
                              _pad_rows(win.reshape(bs, ts, 256), LANES), past_len, ts)
            o_a = _scomb(o_cmp, o_slc, o_win, _pad_rows(misc.reshape(bs, ts, LANES), 8), ts)
            o_a = o_a[:, :ts].reshape(ns_, NSA_HEADS * HEAD_DIM)
            mla_new = jnp.pad(_pad_rows(mla.reshape(bs, ts, MLA_ROW), LANES),
                              ((0, 0), (0, 0), (l * MLA_ROW, (n_ab - 1 - l) * MLA_ROW)))
            o_b = _smla(l, page_table, cache_mla_pages, _pad_rows(qf.reshape(bs, ts, -1), 8),
                        *_pack_w_absorb(mla_w_uk[l], mla_w_uv[l], l, n_ab), mla_new, ts)
            mix_s = jnp.concatenate([o_a, o_b[:, :ts].reshape(ns_, -1)], axis=1)
            nsa_s.append(rows.reshape(bs, ts, 512))
            mla_s.append(mla.reshape(bs, ts, MLA_ROW))
            win_s.append(win.reshape(bs, ts, 256))
        else:
            w_c = w_in_c[l].astype(BF16)
            w_out = w_out_c[l].astype(BF16)

            t64, _, tile = tabs["p"]
            outs = _pre_c(h_p, gm, w_c, t64, tile)
            os_, ls_ = [], []
            for g, (window, dil) in enumerate(DIL_PATTERNS):
                o, lse = _dil_prompt(outs[g], outs[N_DIL + g], bp, tp, dil)
                os_.append(o)
                ls_.append(lse)
                dil_p[g].append(outs[N_DIL + g].reshape(bp, tp, 2 * hw)[:, max(tp - window, 0):])
            mix_p = _dil_merge(os_, ls_, tile)

            t64, _, tile = tabs["s"]
            outs = _pre_c(h_s, gm, w_c, t64, tile)
            q_pad = _pad_rows(jnp.concatenate(outs[:N_DIL], axis=1).reshape(bs, ts, -1), 8)
            new_pad = _pad_rows(jnp.concatenate(outs[N_DIL:], axis=1).reshape(bs, ts, -1), LANES)
            mix_s = _sdil(l, q_pad, new_pad, caches_dil, ts)[:, :ts].reshape(ns_, hw)
            for g in range(N_DIL):
                dil_s[g].append(outs[N_DIL + g].reshape(bs, ts, 2 * hw))

        post_w = (w_out, row2(norm_ffn[i]), w_gate_up[i].astype(BF16), w_down[i].astype(BF16),
                  row2(norm_ple[i]), w_ple_gate[i].astype(BF16))
        wpp, gfin = w_ple_proj[i].astype(BF16), row2(norm_final)
        res = _post(h_p, mix_p, *post_w, p_prompt[i].reshape(np_, PLE_DIM), wpp, gfin, final, tabs["p"][2])
        h_p = res[0]
        if final:
            y_p = res[1]
        res = _post(h_s, mix_s, *post_w, p_sample[i].reshape(ns_, PLE_DIM), wpp, gfin, final, tabs["s"][2])
        h_s = res[0]
        if final:
            y_s = res[1]

    def rows_out(parts, b, t, tail):
        return jnp.stack(parts, axis=2).reshape((b, t, len(parts)) + tail)

    def bufs_out(parts, tail):
        x = jnp.stack(parts, axis=0)
        return x.reshape(x.shape[:3] + tail)

    kv_tail = (4, NSA_KV_HEADS, HEAD_DIM)
    win_tail = (2, NSA_KV_HEADS, HEAD_DIM)
    dil_tail = (2, DIL_HEADS, HEAD_DIM)
    out = [y_p.reshape(bp, tp, D_MODEL), y_s.reshape(bs, ts, D_MODEL),
           rows_out(nsa_p, bp, tp, kv_tail), rows_out(nsa_s, bs, ts, kv_tail),
           rows_out(mla_p, bp, tp, (MLA_ROW,)), rows_out(mla_s, bs, ts, (MLA_ROW,)),
           bufs_out(win_p, win_tail), bufs_out(win_s, win_tail)]
    for g in range(N_DIL):
        out += [bufs_out(dil_p[g], dil_tail), bufs_out(dil_s[g], dil_tail)]
    return tuple(out)
```

```python
import functools

import numpy as np
import jax
import jax.numpy as jnp
from jax import lax
from jax.experimental import pallas as pl
from jax.experimental.pallas import tpu as pltpu

F32 = jnp.float32
BF16 = jnp.bfloat16

D_MODEL = 1024
HEAD_DIM = 64
ROT_DIM = HEAD_DIM // 4
ROPE_THETA = 500000.0
RMS_EPS = 1e-6
NSA_HEADS = 8
NSA_KV_HEADS = 2
NSA_GROUP = NSA_HEADS // NSA_KV_HEADS
CMP_BLOCK = 32
SEL_BLOCK = 64
N_SEL = 16
NSA_WINDOW = 512
MLA_HEADS = 8
Q_LORA = 384
KV_LORA = 256
QK_NOPE = 64
QK_ROPE = 32
V_DIM = 64
MLA_ROW = KV_LORA + QK_ROPE
DIL_PATTERNS = ((128, 1), (512, 4), (2048, 16))
N_DIL = len(DIL_PATTERNS)
DIL_HEADS = 8
DIL_BACK = 128
D_FF = ((-(-8 * D_MODEL // 3) + 255) // 256) * 256
PLE_DIM = 256
AB_SPLITS = (NSA_HEADS * HEAD_DIM, 6 * NSA_KV_HEADS * HEAD_DIM, 3 * NSA_HEADS, Q_LORA, KV_LORA, QK_ROPE)

LANES = 128
NEG = -1e30
V7X_VMEM_LIMIT = 56 * 1024 * 1024
ROW_TILE = 256
FF_CHUNK = 256
NSA_TQ = 256
NSA_TK = 512
MLA_TQ = 512
MLA_TK = 512
PAGES_PER_STEP = 8
SMLA_SLOTS = 16


def _dot(a, b):
    return jnp.dot(a.astype(BF16), b.astype(BF16), preferred_element_type=F32)


def _dot_nt(a, b):
    return lax.dot_general(a.astype(BF16), b.astype(BF16), (((1,), (1,)), ((), ())),
                           preferred_element_type=F32)


def _rms(x, g):
    return x * lax.rsqrt(jnp.mean(x * x, axis=-1, keepdims=True) + RMS_EPS) * g


def _tile_lanes(a, reps):
    return a if reps == 1 else jnp.concatenate([a] * reps, axis=1)


def _rope(x, cos, sin_up, sin_dn, half):
    w = x.shape[-1]
    reps = w // LANES
    return (x * _tile_lanes(cos, reps)
            + pltpu.roll(x, half, 1) * _tile_lanes(sin_up, reps)
            + pltpu.roll(x, w - half, 1) * _tile_lanes(sin_dn, reps))


def _rope_tables(pos, rot_dim, period, offset):
    half = rot_dim // 2
    inv = jnp.power(ROPE_THETA, -jnp.arange(half, dtype=F32) / half)
    ang = pos.astype(F32)[:, None] * inv[None]
    cos, sin = jnp.cos(ang), jnp.sin(ang)
    lane = np.arange(LANES) % period - offset
    lo = (lane >= 0) & (lane < half)
    hi = (lane >= half) & (lane < rot_dim)
    j = np.clip(np.where(hi, lane - half, lane), 0, half - 1)
    cos_t = jnp.where(lo | hi, cos[:, j], 1.0)
    sin_up = jnp.where(hi, sin[:, j], 0.0)
    sin_dn = jnp.where(lo, -sin[:, j], 0.0)
    return cos_t, sin_up, sin_dn


def _softmax_cols(pieces):
    m = None
    for s, mask in pieces:
        mm = jnp.max(jnp.where(mask, s, NEG), axis=0, keepdims=True)
        m = mm if m is None else jnp.maximum(m, mm)
    es = [jnp.where(mask, jnp.exp(s - m), 0.0) for s, mask in pieces]
    den = sum(jnp.sum(e, axis=0, keepdims=True) for e in es)
    inv = 1.0 / jnp.maximum(den, 1e-30)
    return [e * inv for e in es], den, m


def _softmax_rows(pieces):
    m = None
    for s, mask in pieces:
        mm = jnp.max(jnp.where(mask, s, NEG), axis=1, keepdims=True)
        m = mm if m is None else jnp.maximum(m, mm)
    es = [jnp.where(mask, jnp.exp(s - m), 0.0) for s, mask in pieces]
    den = sum(jnp.sum(e, axis=1, keepdims=True) for e in es)
    inv = 1.0 / jnp.maximum(den, 1e-30)
    return [e * inv for e in es], den, m


def _flash_update(carry, s_t, mask, v_t):
    m, l, acc = carry
    m_new = jnp.maximum(m, jnp.max(jnp.where(mask, s_t, NEG), axis=0, keepdims=True))
    p = jnp.where(mask, jnp.exp(s_t - m_new), 0.0)
    alpha = jnp.exp(m - m_new)
    l_new = alpha * l + jnp.sum(p, axis=0, keepdims=True)
    acc_new = alpha * acc + jnp.dot(v_t, p.astype(BF16), preferred_element_type=F32)
    return m_new, l_new, acc_new


def _params(n_grid):
    return pltpu.CompilerParams(dimension_semantics=("arbitrary",) * n_grid,
                                vmem_limit_bytes=V7X_VMEM_LIMIT)


def _resident(shape):
    nd = len(shape)
    return pl.BlockSpec(shape, lambda *_: (0,) * nd, pipeline_mode=pl.Buffered(1))


def _rows(width, tile):
    return pl.BlockSpec((tile, width), lambda i: (i, 0))


def _pre_ab_kernel(h_ref, gm_ref, w_ref, gq_ref, gkv_ref, wuq_ref,
                   c64_ref, u64_ref, d64_ref, cm_ref, um_ref, dm_ref,
                   qc_ref, qr_ref, rows_ref, win_ref, qf_ref, mla_ref, misc_ref):
    t64 = (c64_ref[...], u64_ref[...], d64_ref[...])
    tm_ = (cm_ref[...], um_ref[...], dm_ref[...])
    hn = _rms(h_ref[...], gm_ref[...])
    z = _dot(hn, w_ref[...])
    nq = NSA_HEADS * LANES
    qc = z[:, 0:nq] * (HEAD_DIM ** -0.5)
    qc_ref[...] = qc
    qr_ref[...] = _rope(qc, *t64, ROT_DIM // 2)
    kv = z[:, nq:nq + 768]
    rows_ref[:, 0:256] = kv[:, 0:256]
    rows_ref[:, 256:384] = _rope(kv[:, 256:384], *t64, ROT_DIM // 2)
    rows_ref[:, 384:512] = kv[:, 384:512]
    win_ref[:, 0:128] = _rope(kv[:, 512:640], *t64, ROT_DIM // 2)
    win_ref[:, 128:256] = kv[:, 640:768]
    o = nq + 768
    qn = _rms(z[:, o:o + Q_LORA], gq_ref[...])
    qf_ref[...] = _rope(_dot(qn, wuq_ref[...]), *tm_, QK_ROPE // 2)
    o += Q_LORA
    ckv = _rms(z[:, o:o + KV_LORA], gkv_ref[...])
    o += KV_LORA
    misc = _rope(z[:, o:o + LANES], *tm_, QK_ROPE // 2)
    mla_ref[:, 0:KV_LORA] = ckv
    mla_ref[:, KV_LORA:MLA_ROW] = misc[:, 0:QK_ROPE]
    misc_ref[...] = misc


def _pre_ab(h, gm, w, gq, gkv, wuq, tabs64, tabsm, tile):
    n = h.shape[0]
    t_tab = tabs64[0].shape[0]
    nt = t_tab // tile
    tab = pl.BlockSpec((tile, LANES), lambda i: (i % nt, 0))
    widths = (NSA_HEADS * LANES, NSA_HEADS * LANES, 512, 256, MLA_HEADS * LANES, MLA_ROW, LANES)
    return pl.pallas_call(
        _pre_ab_kernel,
        grid=(n // tile,),
        in_specs=[_rows(D_MODEL, tile), _resident(gm.shape), _resident(w.shape), _resident(gq.shape),
                  _resident(gkv.shape), _resident(wuq.shape)] + [tab] * 6,
        out_specs=[_rows(wd, tile) for wd in widths],
        out_shape=[jax.ShapeDtypeStruct((n, wd), F32) for wd in widths],
        compiler_params=_params(1),
        name="pre_ab",
    )(h, gm, w, gq, gkv, wuq, *tabs64, *tabsm)


def _mla_kv_kernel(mla_ref, misc_ref, w_ref, k_ref, v_ref):
    x = jnp.concatenate([mla_ref[:, 0:KV_LORA], misc_ref[...]], axis=1)
    z = _dot(x, w_ref[...])
    k_ref[...] = z[:, 0:MLA_HEADS * LANES]
    v_ref[...] = z[:, MLA_HEADS * LANES:]


def _mla_kv(mla, misc, w, tile):
    n = mla.shape[0]
    return pl.pallas_call(
        _mla_kv_kernel,
        grid=(n // tile,),
        in_specs=[_rows(MLA_ROW, tile), _rows(LANES, tile), _resident(w.shape)],
        out_specs=[_rows(MLA_HEADS * LANES, tile), _rows(MLA_HEADS * V_DIM, tile)],
        out_shape=[jax.ShapeDtypeStruct((n, MLA_HEADS * LANES), F32),
                   jax.ShapeDtypeStruct((n, MLA_HEADS * V_DIM), F32)],
        compiler_params=_params(1),
        name="mla_kv",
    )(mla, misc, w)


def _pre_c_kernel(h_ref, gm_ref, w_ref, c64_ref, u64_ref, d64_ref, *out_refs):
    t64 = (c64_ref[...], u64_ref[...], d64_ref[...])
    hn = _rms(h_ref[...], gm_ref[...])
    z = _dot(hn, w_ref[...])
    hw = DIL_HEADS * HEAD_DIM
    for g in range(N_DIL):
        o = g * 3 * hw
        out_refs[g][...] = _rope(z[:, o:o + hw] * (HEAD_DIM ** -0.5), *t64, ROT_DIM // 2)
        out_refs[N_DIL + g][:, 0:hw] = _rope(z[:, o + hw:o + 2 * hw], *t64, ROT_DIM // 2)
        out_refs[N_DIL + g][:, hw:2 * hw] = z[:, o + 2 * hw:o + 3 * hw]


def _pre_c(h, gm, w, tabs64, tile):
    n = h.shape[0]
    nt = tabs64[0].shape[0] // tile
    tab = pl.BlockSpec((tile, LANES), lambda i: (i % nt, 0))
    hw = DIL_HEADS * HEAD_DIM
    widths = (hw,) * N_DIL + (2 * hw,) * N_DIL
    return pl.pallas_call(
        _pre_c_kernel,
        grid=(n // tile,),
        in_specs=[_rows(D_MODEL, tile), _resident(gm.shape), _resident(w.shape)] + [tab] * 3,
        out_specs=[_rows(wd, tile) for wd in widths],
        out_shape=[jax.ShapeDtypeStruct((n, wd), F32) for wd in widths],
        compiler_params=_params(1),
        name="pre_c",
    )(h, gm, w, *tabs64)


def _post_kernel(final, h_ref, a_ref, wout_ref, gffn_ref, wgu_ref, wd_ref, gple_ref, wpg_ref,
                 ple_ref, wpp_ref, gfin_ref, o_ref, *y_ref):
    h1 = h_ref[...] + _dot(a_ref[...], wout_ref[...])
    xn = _rms(h1, gffn_ref[...]).astype(BF16)
    acc = jnp.zeros(h1.shape, F32)
    for c in range(D_FF // FF_CHUNK):
        lo = c * FF_CHUNK
        g = jnp.dot(xn, wgu_ref[:, lo:lo + FF_CHUNK], preferred_element_type=F32)
        u = jnp.dot(xn, wgu_ref[:, D_FF + lo:D_FF + lo + FF_CHUNK], preferred_element_type=F32)
        a = g * jax.nn.sigmoid(g) * u
        acc = acc + jnp.dot(a.astype(BF16), wd_ref[lo:lo + FF_CHUNK, :], preferred_element_type=F32)
    h2 = h1 + acc
    gate = jax.nn.sigmoid(_dot(_rms(h2, gple_ref[...]), wpg_ref[...]))
    h3 = h2 + gate * _dot(ple_ref[...], wpp_ref[...])
    o_ref[...] = h3
    if final:
        y_ref[0][...] = _rms(h3, gfin_ref[...])


def _post(h, a, wout, gffn, wgu, wd, gple, wpg, ple, wpp, gfin, final, tile):
    n = h.shape[0]
    n_out = 2 if final else 1
    return pl.pallas_call(
        functools.partial(_post_kernel, final),
        grid=(n // tile,),
        in_specs=[_rows(D_MODEL, tile), _rows(a.shape[1], tile), _resident(wout.shape), _resident(gffn.shape),
                  _resident(wgu.shape), _resident(wd.shape), _resident(gple.shape), _resident(wpg.shape),
                  _rows(PLE_DIM, tile), _resident(wpp.shape), _resident(gfin.shape)],
        out_specs=[_rows(D_MODEL, tile)] * n_out,
        out_shape=[jax.ShapeDtypeStruct((n, D_MODEL), F32)] * n_out,
        compiler_params=_params(1),
        name="post",
    )(h, a, wout, gffn, wgu, wd, gple, wpg, ple, wpp, gfin)


def _cmp_kernel(rows_ref, pe_ref, phik_ref, phiv_ref, kc_ref, vc_ref):
    t = rows_ref.shape[1]
    nsb = t // SEL_BLOCK
    x3 = rows_ref[0, :, 0:256].reshape(nsb, SEL_BLOCK, 256)
    pe = pe_ref[...][None]
    even = jnp.mean(x3[:, 0:CMP_BLOCK, :] + pe, axis=1)
    odd = jnp.mean(x3[:, CMP_BLOCK:SEL_BLOCK, :] + pe, axis=1)
    km = jnp.concatenate([even, odd], axis=0)
    kc_ref[0] = _dot(km[:, 0:128], phik_ref[...])
    vc_ref[0] = _dot(km[:, 128:256], phiv_ref[...])


def _cmp_prompt(rows3, pe, phik, phiv):
    b, t, _ = rows3.shape
    ncb = t // CMP_BLOCK
    return pl.pallas_call(
        _cmp_kernel,
        grid=(b,),
        in_specs=[pl.BlockSpec((1, t, 512), lambda i: (i, 0, 0)), _resident(pe.shape),
                  _resident(phik.shape), _resident(phiv.shape)],
        out_specs=[pl.BlockSpec((1, ncb, LANES), lambda i: (i, 0, 0))] * 2,
        out_shape=[jax.ShapeDtypeStruct((b, ncb, LANES), F32)] * 2,
        compiler_params=_params(1),
        name="nsa_cmp",
    )(rows3, pe, phik, phiv)


def _nsa_prompt_kernel(qc_ref, qr_ref, rows_ref, win_ref, kc_ref, vc_ref, misc_ref, exp_ref, o_ref):
    qi = pl.program_id(1)
    tq = qc_ref.shape[1]
    t_len = rows_ref.shape[1]
    nsb = t_len // SEL_BLOCK
    ncb = 2 * nsb
    q0 = qi * tq
    gates_t = jax.nn.sigmoid(misc_ref[0].T)
    kc = kc_ref[0].astype(BF16)
    vc_t = vc_ref[0].T.astype(BF16)
    qpos1 = q0 + lax.broadcasted_iota(jnp.int32, (1, tq), 1)
    qpos4 = _tile_lanes(qpos1, NSA_GROUP)
    out_rows = []
    for g in range(NSA_KV_HEADS):
        heads = range(g * NSA_GROUP, (g + 1) * NSA_GROUP)
        qc_g = jnp.concatenate([qc_ref[0, :, h * LANES:(h + 1) * LANES] for h in heads], axis=0).astype(BF16)
        qr_g = jnp.concatenate([qr_ref[0, :, h * LANES:(h + 1) * LANES] for h in heads], axis=0).astype(BF16)
        s_c = _dot_nt(kc, qc_g)
        n_idx = lax.broadcasted_iota(jnp.int32, (ncb, NSA_GROUP * tq), 0)
        n_orig = jnp.where(n_idx < nsb, 2 * n_idx, 2 * (n_idx - nsb) + 1)
        cmask = (n_orig + 1) * CMP_BLOCK - 1 <= qpos4
        (p_c,), _, _ = _softmax_cols([(s_c, cmask)])
        o_c = jnp.dot(vc_t, p_c.astype(BF16), preferred_element_type=F32)[g * HEAD_DIM:(g + 1) * HEAD_DIM]
        imp = sum(p_c[:, r * tq:(r + 1) * tq] for r in range(NSA_GROUP))
        imp = imp[0:nsb] + imp[nsb:ncb]
        blk = lax.broadcasted_iota(jnp.int32, (nsb, tq), 0)
        forced = (blk == 0) | (blk == (qpos1 >> 6))
        val = jnp.where(forced, jnp.inf, jnp.where(blk * SEL_BLOCK <= qpos1, imp, -jnp.inf))
        rank = jnp.zeros((nsb, tq), F32)
        for i in range(nsb):
            vi = val[i:i + 1, :]
            rank = rank + jnp.where(vi > val, 1.0, 0.0) + jnp.where(vi == val, jnp.where(blk > i, 1.0, 0.0), 0.0)
        sel = jnp.where(rank < min(N_SEL, nsb), jnp.where(val > -jnp.inf, 1.0, 0.0), 0.0).astype(BF16)

        init = (jnp.full((1, NSA_GROUP * tq), NEG, F32), jnp.zeros((1, NSA_GROUP * tq), F32),
                jnp.zeros((HEAD_DIM, NSA_GROUP * tq), F32))

        def slc_body(kt, carry):
            k0 = pl.multiple_of(kt * NSA_TK, NSA_TK)
            k = rows_ref[0, pl.ds(k0, NSA_TK), 256:384]
            v = rows_ref[0, pl.ds(k0, NSA_TK), 384:512]
            s_t = _dot_nt(k, qr_g)
            chosen = jnp.dot(exp_ref[pl.ds(k0, NSA_TK), :], sel, preferred_element_type=F32)
            kpos = k0 + lax.broadcasted_iota(jnp.int32, (NSA_TK, tq), 0)
            mask = _tile_lanes((chosen > 0.5) & (kpos <= qpos1), NSA_GROUP)
            v_t = v.T[g * HEAD_DIM:(g + 1) * HEAD_DIM].astype(BF16)
            return _flash_update(carry, s_t, mask, v_t)

        n_kt = (q0 + tq - 1) // NSA_TK + 1
        _, l_s, acc_s = lax.fori_loop(0, n_kt, slc_body, init)
        o_s = acc_s / l_s

        carry = init
        n_back_tiles = -(-(NSA_WINDOW - 1) // tq)
        for j in range(n_back_tiles + 1):
            kt = qi - n_back_tiles + j
            k0 = pl.multiple_of(jnp.maximum(kt, 0) * tq, tq)
            k = win_ref[0, pl.ds(k0, tq), 0:128]
            v = win_ref[0, pl.ds(k0, tq), 128:256]
            s_t = _dot_nt(k, qr_g)
            kpos = kt * tq + lax.broadcasted_iota(jnp.int32, (tq, tq), 0)
            mask = _tile_lanes((kpos <= qpos1) & (kpos >= qpos1 - (NSA_WINDOW - 1)) & (kpos >= 0), NSA_GROUP)
            v_t = v.T[g * HEAD_DIM:(g + 1) * HEAD_DIM].astype(BF16)
            carry = _flash_update(carry, s_t, mask, v_t)
        o_w = carry[2] / carry[1]

        for r, h in enumerate(heads):
            sl = slice(r * tq, (r + 1) * tq)
            row = 32 + h
            out_rows.append(gates_t[row:row + 1] * o_c[:, sl]
                            + gates_t[row + 8:row + 9] * o_s[:, sl]
                            + gates_t[row + 16:row + 17] * o_w[:, sl])
    o_ref[0] = jnp.concatenate(out_rows, axis=0).T


def _nsa_prompt(qc3, qr3, rows3, win3, kc, vc, misc3, expand):
    b, t, _ = rows3.shape
    tq = min(NSA_TQ, t)
    ncb = kc.shape[1]
    qspec = pl.BlockSpec((1, tq, NSA_HEADS * LANES), lambda i, j: (i, j, 0))
    return pl.pallas_call(
        _nsa_prompt_kernel,
        grid=(b, t // tq),
        in_specs=[qspec, qspec,
                  pl.BlockSpec((1, t, 512), lambda i, j: (i, 0, 0)),
                  pl.BlockSpec((1, t, 256), lambda i, j: (i, 0, 0)),
                  pl.BlockSpec((1, ncb, LANES), lambda i, j: (i, 0, 0)),
                  pl.BlockSpec((1, ncb, LANES), lambda i, j: (i, 0, 0)),
                  pl.BlockSpec((1, tq, LANES), lambda i, j: (i, j, 0)),
                  _resident(expand.shape)],
        out_specs=pl.BlockSpec((1, tq, NSA_HEADS * HEAD_DIM), lambda i, j: (i, j, 0)),
        out_shape=jax.ShapeDtypeStruct((b, t, NSA_HEADS * HEAD_DIM), F32),
        compiler_params=_params(2),
        name="nsa_prompt",
    )(qc3, qr3, rows3, win3, kc, vc, misc3, expand)


def _mla_prompt_kernel(q_ref, k_ref, v_ref, o_ref):
    qi = pl.program_id(2)
    tq = q_ref.shape[1]
    tk = min(MLA_TK, k_ref.shape[1])
    q0 = qi * tq
    qpos = q0 + lax.broadcasted_iota(jnp.int32, (1, tq), 1)
    scale = (QK_NOPE + QK_ROPE) ** -0.5
    outs = []
    for hh in range(2):
        q = q_ref[0, :, hh * LANES:(hh + 1) * LANES].astype(BF16)

        def body(kt, carry):
            k0 = pl.multiple_of(kt * tk, tk)
            k = k_ref[0, pl.ds(k0, tk), hh * LANES:(hh + 1) * LANES]
            s_t = _dot_nt(k, q) * scale
            kpos = k0 + lax.broadcasted_iota(jnp.int32, (tk, tq), 0)
            v_t = v_ref[0, pl.ds(k0, tk), :].T[hh * V_DIM:(hh + 1) * V_DIM].astype(BF16)
            return _flash_update(carry, s_t, kpos <= qpos, v_t)

        init = (jnp.full((1, tq), NEG, F32), jnp.zeros((1, tq), F32), jnp.zeros((V_DIM, tq), F32))
        _, l, acc = lax.fori_loop(0, (q0 + tq - 1) // tk + 1, body, init)
        outs.append(acc / l)
    o_ref[0] = jnp.concatenate(outs, axis=0).T


def _mla_prompt(qf3, kf3, v3):
    b, t, _ = qf3.shape
    tq = min(MLA_TQ, t)
    return pl.pallas_call(
        _mla_prompt_kernel,
        grid=(b, MLA_HEADS // 2, t // tq),
        in_specs=[pl.BlockSpec((1, tq, 2 * LANES), lambda i, p, j: (i, j, p)),
                  pl.BlockSpec((1, t, 2 * LANES), lambda i, p, j: (i, 0, p)),
                  pl.BlockSpec((1, t, 2 * V_DIM), lambda i, p, j: (i, 0, p))],
        out_specs=pl.BlockSpec((1, tq, 2 * V_DIM), lambda i, p, j: (i, j, p)),
        out_shape=jax.ShapeDtypeStruct((b, t, MLA_HEADS * V_DIM), F32),
        compiler_params=_params(3),
        name="mla_prompt",
    )(qf3, kf3, v3)


def _dil_prompt_kernel(sub, has_prev, q_ref, kc_ref, vc_ref, kp_ref, vp_ref, o_ref, lse_ref):
    i = pl.program_id(2)
    tq = q_ref.shape[1]
    lane = lax.broadcasted_iota(jnp.int32, (1, LANES), 1)
    half_masks = [jnp.where(lane < HEAD_DIM, 1.0, 0.0), jnp.where(lane >= HEAD_DIM, 1.0, 0.0)]
    n_keys = 2 * sub if has_prev else sub
    c_idx = lax.broadcasted_iota(jnp.int32, (n_keys, sub), 0)
    q_idx = lax.broadcasted_iota(jnp.int32, (n_keys, sub), 1)
    kpos = c_idx - sub if has_prev else c_idx
    band = (kpos <= q_idx) & (kpos >= q_idx - DIL_BACK)
    for s in range(tq // sub):
        rs = slice(s * sub, (s + 1) * sub)
        q = q_ref[0, rs, :]
        k = kc_ref[0, rs, :]
        v = vc_ref[0, rs, :]
        mask = band
        if has_prev:
            if s == 0:
                k = jnp.concatenate([kp_ref[0], k], axis=0)
                v = jnp.concatenate([vp_ref[0], v], axis=0)
                mask = band & (c_idx >= jnp.where(i > 0, 0, sub))
            else:
                ps = slice((s - 1) * sub, s * sub)
                k = jnp.concatenate([kc_ref[0, ps, :], k], axis=0)
                v = jnp.concatenate([vc_ref[0, ps, :], v], axis=0)
        o_rows, lse_rows = [], []
        for j in range(DIL_HEADS // 2):
            cs = slice(j * LANES, (j + 1) * LANES)
            v_t = v[:, cs].T.astype(BF16)
            k_j = k[:, cs].astype(BF16)
            for hh in range(2):
                s_t = _dot_nt(k_j, q[:, cs] * half_masks[hh])
                (p,), den, m = _softmax_cols([(s_t, mask)])
                o_rows.append(jnp.dot(v_t[hh * HEAD_DIM:(hh + 1) * HEAD_DIM], p.astype(BF16),
                                      preferred_element_type=F32))
                lse_rows.append(jnp.broadcast_to(jnp.log(den) + m, (HEAD_DIM, sub)))
        o_ref[0, rs, :] = jnp.concatenate(o_rows, axis=0).T
        lse_ref[0, rs, :] = jnp.concatenate(lse_rows, axis=0).T


def _dil_prompt(q, kv, b, t, dil):
    hw = DIL_HEADS * HEAD_DIM
    l = t // dil
    sub = min(LANES, l)
    has_prev = l > sub
    tq = min(l, 4 * sub)
    per = tq // sub
    q3 = q.reshape(b, l, dil * hw)
    kv3 = kv.reshape(b, l, dil * 2 * hw)
    cur = lambda off: pl.BlockSpec((1, tq, hw), lambda bi, r, i: (bi, i, 2 * r + off))
    prev = lambda off: pl.BlockSpec((1, sub, hw), lambda bi, r, i: (bi, jnp.maximum(i * per - 1, 0), 2 * r + off))
    qo = pl.BlockSpec((1, tq, hw), lambda bi, r, i: (bi, i, r))
    o, lse = pl.pallas_call(
        functools.partial(_dil_prompt_kernel, sub, has_prev),
        grid=(b, dil, l // tq),
        in_specs=[qo, cur(0), cur(1), prev(0), prev(1)],
        out_specs=[qo, qo],
        out_shape=[jax.ShapeDtypeStruct((b, l, dil * hw), F32)] * 2,
        compiler_params=_params(3),
        name="dil_prompt",
    )(q3, kv3, kv3, kv3, kv3)
    return o.reshape(b * t, hw), lse.reshape(b * t, hw)


def _dil_merge_kernel(o0, o1, o2, l0, l1, l2, out_ref):
    ls = [l0[...], l1[...], l2[...]]
    mx = jnp.maximum(jnp.maximum(ls[0], ls[1]), ls[2])
    ws = [jnp.exp(x - mx) for x in ls]
    num = ws[0] * o0[...] + ws[1] * o1[...] + ws[2] * o2[...]
    out_ref[...] = num / (ws[0] + ws[1] + ws[2])


def _dil_merge(outs, lses, tile):
    n, hw = outs[0].shape
    return pl.pallas_call(
        _dil_merge_kernel,
        grid=(n // tile,),
        in_specs=[_rows(hw, tile)] * 6,
        out_specs=_rows(hw, tile),
        out_shape=jax.ShapeDtypeStruct((n, hw), F32),
        compiler_params=_params(1),
        name="dil_merge",
    )(*outs, *lses)


def _snsa_a_kernel(layer, past_len, t_new, pt_ref, *refs):
    pages = refs[:PAGES_PER_STEP]
    qc_ref, pe_ref, phik_ref, phiv_ref, ocmp_ref, idx_ref, cme_ref, cmo_ref = refs[PAGES_PER_STEP:]
    del layer
    step = pl.program_id(1)
    n_steps = pl.num_programs(1)
    pe = pe_ref[...][None]
    for quad in range(PAGES_PER_STEP // 4):
        x = jnp.concatenate([pages[4 * quad + j][0, 0].T for j in range(4)], axis=0)
        x3 = x.reshape(8, SEL_BLOCK, 256)
        r0 = pl.multiple_of(step * (2 * PAGES_PER_STEP) + quad * 8, 8)
        cme_ref[pl.ds(r0, 8), :] = jnp.mean(x3[:, 0:CMP_BLOCK, :] + pe, axis=1)
        cmo_ref[pl.ds(r0, 8), :] = jnp.mean(x3[:, CMP_BLOCK:SEL_BLOCK, :] + pe, axis=1)

    @pl.when(step == n_steps - 1)
    def _():
        nsb_past = cme_ref.shape[0]
        ncb = 2 * nsb_past
        km = jnp.concatenate([cme_ref[...], cmo_ref[...]], axis=0)
        kc = _dot(km[:, 0:128], phik_ref[...]).astype(BF16)
        vc = _dot(km[:, 128:256], phiv_ref[...]).astype(BF16)
        rows = NSA_GROUP * 8
        t_row = lax.broadcasted_iota(jnp.int32, (rows, 1), 0) & 7
        pos = past_len + t_row
        n_idx = lax.broadcasted_iota(jnp.int32, (rows, ncb), 1)
        n_orig = jnp.where(n_idx < nsb_past, 2 * n_idx, 2 * (n_idx - nsb_past) + 1)
        cmask = (n_orig + 1) * CMP_BLOCK - 1 <= pos
        nsb = -(-(past_len + t_new) // SEL_BLOCK)
        wide = -(-nsb // LANES) * LANES
        blk = lax.broadcasted_iota(jnp.int32, (8, wide), 1)
        blk_f = blk.astype(F32)
        pos8 = past_len + lax.broadcasted_iota(jnp.int32, (8, 1), 0)
        lane = lax.broadcasted_iota(jnp.int32, (8, LANES), 1)
        for g in range(NSA_KV_HEADS):
            s = _dot_nt(qc_ref[0, g], kc)
            (p,), _, _ = _softmax_rows([(s, cmask)])
            ocmp_ref[0, g] = jnp.dot(p.astype(BF16), vc, preferred_element_type=F32)
            imp = sum(p[r * 8:(r + 1) * 8, 0:nsb_past] + p[r * 8:(r + 1) * 8, nsb_past:ncb]
                      for r in range(NSA_GROUP))
            imp = jnp.concatenate([imp, jnp.zeros((8, wide - nsb_past), F32)], axis=1)
            forced = (blk == 0) | (blk == (pos8 >> 6))
            val = jnp.where(forced, jnp.inf, jnp.where(blk * SEL_BLOCK <= pos8, imp, -jnp.inf))
            picks = jnp.full((8, LANES), -1.0, F32)
            for k in range(min(N_SEL, nsb)):
                mx = jnp.max(val, axis=1, keepdims=True)
                first = jnp.min(jnp.where(val == mx, blk_f, 1e9), axis=1, keepdims=True)
                picks = jnp.where(lane == k, jnp.where(mx > -jnp.inf, first, -1.0), picks)
                val = jnp.where(blk_f == first, -jnp.inf, val)
            idx_ref[0, g] = picks.astype(jnp.int32)


def _snsa_a(layer, pt, cache_v, qc_st, pe, phik, phiv, t_new):
    b, n_pages = pt.shape
    past_len = n_pages * LANES
    nsb_past = past_len // SEL_BLOCK
    page = lambda j: pl.BlockSpec(
        (1, 1, 256, LANES), lambda bi, s, pt_ref: (pt_ref[bi, s * PAGES_PER_STEP + j], layer, 0, 0))
    const = lambda shape: pl.BlockSpec(shape, lambda bi, s, pt_ref: (0,) * len(shape))
    grid_spec = pltpu.PrefetchScalarGridSpec(
        num_scalar_prefetch=1,
        grid=(b, n_pages // PAGES_PER_STEP),
        in_specs=[page(j) for j in range(PAGES_PER_STEP)] + [
            pl.BlockSpec((1, NSA_KV_HEADS, NSA_GROUP * 8, LANES), lambda bi, s, pt_ref: (bi, 0, 0, 0)),
            const(pe.shape), const(phik.shape), const(phiv.shape)],
        out_specs=[pl.BlockSpec((1, NSA_KV_HEADS, NSA_GROUP * 8, LANES), lambda bi, s, pt_ref: (bi, 0, 0, 0)),
                   pl.BlockSpec((1, NSA_KV_HEADS, 8, LANES), lambda bi, s, pt_ref: (bi, 0, 0, 0))],
        scratch_shapes=[pltpu.VMEM((nsb_past, 256), F32), pltpu.VMEM((nsb_past, 256), F32)],
    )
    return pl.pallas_call(
        functools.partial(_snsa_a_kernel, layer, past_len, t_new),
        grid_spec=grid_spec,
        out_shape=[jax.ShapeDtypeStruct((b, NSA_KV_HEADS, NSA_GROUP * 8, LANES), F32),
                   jax.ShapeDtypeStruct((b, NSA_KV_HEADS, 8, LANES), jnp.int32)],
        compiler_params=_params(2),
        name="snsa_cmp_topk",
    )(pt, *([cache_v] * PAGES_PER_STEP), qc_st, pe, phik, phiv)


def _snsa_b_kernel(past_len, t_new, n_t, pt_ref, sidx_ref, *refs):
    n_sel = N_SEL
    blocks = refs[:n_sel]
    q_ref, new_ref, o_ref = refs[n_sel:]
    del pt_ref
    bi, ti, gi = pl.program_id(0), pl.program_id(1), pl.program_id(2)
    base = ((bi * n_t + ti) * NSA_KV_HEADS + gi) * n_sel
    new_blk = past_len // SEL_BLOCK
    pos = past_len + ti
    q = q_ref[0, 0, 0].astype(BF16)
    k_new = new_ref[0, 0:128, :]
    v_new = new_ref[0, 128:256, :]
    lane = lax.broadcasted_iota(jnp.int32, (8, LANES), 1)
    pieces, vals = [], []
    for j in range(n_sel):
        idx = sidx_ref[base + j]
        is_new = idx == new_blk
        blk = blocks[j][0, 0]
        k_t = jnp.where(is_new, k_new, blk[0:128]).astype(BF16)
        vals.append(jnp.where(is_new, v_new, blk[128:256]).astype(BF16))
        s = jnp.dot(q, k_t, preferred_element_type=F32)
        idx_v = lane * 0 + idx
        kpos = (idx_v >> 1) * LANES + lane
        in_block = (lane >> 6) == (idx_v & 1)
        pieces.append((s, (idx_v >= 0) & in_block & (kpos <= pos) & (kpos < past_len + t_new)))
    ps, _, _ = _softmax_rows(pieces)
    o_ref[0, 0, 0] = sum(_dot_nt(p, v) for p, v in zip(ps, vals))


def _snsa_b(layer, pt, sidx, cache_t, q_tg, new_blk_t, t_new):
    b, n_pages = pt.shape
    past_len = n_pages * LANES
    n_cache_blk = past_len // SEL_BLOCK

    def blk_spec(j):
        def imap(bi, ti, gi, pt_ref, sidx_ref):
            idx = sidx_ref[((bi * t_new + ti) * NSA_KV_HEADS + gi) * N_SEL + j]
            idx = jnp.clip(idx, 0, n_cache_blk - 1)
            return (pt_ref[bi, idx // 2], layer, 1, 0)
        return pl.BlockSpec((1, 1, 256, LANES), imap)

    grid_spec = pltpu.PrefetchScalarGridSpec(
        num_scalar_prefetch=2,
        grid=(b, t_new, NSA_KV_HEADS),
        in_specs=[blk_spec(j) for j in range(N_SEL)] + [
            pl.BlockSpec((1, 1, 1, 8, LANES), lambda bi, ti, gi, *_: (bi, ti, gi, 0, 0)),
            pl.BlockSpec((1, 256, LANES), lambda bi, ti, gi, *_: (bi, 0, 0))],
        out_specs=pl.BlockSpec((1, 1, 1, 8, LANES), lambda bi, ti, gi, *_: (bi, ti, gi, 0, 0)),
    )
    return pl.pallas_call(
        functools.partial(_snsa_b_kernel, past_len, t_new, t_new),
        grid_spec=grid_spec,
        out_shape=jax.ShapeDtypeStruct((b, t_new, NSA_KV_HEADS, 8, LANES), F32),
        compiler_params=_params(3),
        name="snsa_slc",
    )(pt, sidx, *([cache_t] * N_SEL), q_tg, new_blk_t)


def _snsa_win_kernel(past_len, t_new, q_ref, c_ref, new_ref, o_ref):
    lw = c_ref.shape[3]
    rows = NSA_GROUP * 8
    t_row = lax.broadcasted_iota(jnp.int32, (rows, 1), 0) & 7
    pos = past_len + t_row
    kpos_c = (past_len - lw) + lax.broadcasted_iota(jnp.int32, (rows, lw), 1)
    c_new = lax.broadcasted_iota(jnp.int32, (rows, LANES), 1)
    kpos_n = past_len + c_new
    mask_c = (kpos_c <= pos) & (kpos_c >= pos - (NSA_WINDOW - 1))
    mask_n = (c_new < t_new) & (kpos_n <= pos) & (kpos_n >= pos - (NSA_WINDOW - 1))
    kc, vc = c_ref[0, 0, 0:128, :], c_ref[0, 0, 128:256, :]
    kn, vn = new_ref[0, 0:128, :], new_ref[0, 128:256, :]
    for g in range(NSA_KV_HEADS):
        q = q_ref[0, g]
        (p1, p2), _, _ = _softmax_rows([(_dot(q, kc), mask_c), (_dot(q, kn), mask_n)])
        o_ref[0, g] = _dot_nt(p1, vc) + _dot_nt(p2, vn)


def _snsa_win(layer, qr_st, cache_win, win_new, past_len, t_new):
    b = qr_st.shape[0]
    lw = cache_win.shape[3]
    return pl.pallas_call(
        functools.partial(_snsa_win_kernel, past_len, t_new),
        grid=(b,),
        in_specs=[pl.BlockSpec((1, NSA_KV_HEADS, NSA_GROUP * 8, LANES), lambda i: (i, 0, 0, 0)),
                  pl.BlockSpec((1, 1, 256, lw), lambda i: (layer, i, 0, 0)),
                  pl.BlockSpec((1, 256, LANES), lambda i: (i, 0, 0))],
        out_specs=pl.BlockSpec((1, NSA_KV_HEADS, NSA_GROUP * 8, LANES), lambda i: (i, 0, 0, 0)),
        out_shape=jax.ShapeDtypeStruct((b, NSA_KV_HEADS, NSA_GROUP * 8, LANES), F32),
        compiler_params=_params(1),
        name="snsa_win",
    )(qr_st, cache_win, win_new)


def _scomb_kernel(t_new, oc_ref, os_ref, ow_ref, misc_ref, o_ref):
    gates_t = jax.nn.sigmoid(misc_ref[0].T)
    pad = jnp.zeros((8 - t_new, LANES), F32)
    rows = []
    for g in range(NSA_KV_HEADS):
        gs = slice(g * HEAD_DIM, (g + 1) * HEAD_DIM)
        for r in range(NSA_GROUP):
            h = g * NSA_GROUP + r
            oc = oc_ref[0, g, r * 8:(r + 1) * 8, :].T[gs]
            ow = ow_ref[0, g, r * 8:(r + 1) * 8, :].T[gs]
            os_ = jnp.concatenate([os_ref[0, t, g, r:r + 1, :] for t in range(t_new)] + [pad], axis=0).T[gs]
            row = 32 + h
            rows.append(gates_t[row:row + 1] * oc + gates_t[row + 8:row + 9] * os_ + gates_t[row + 16:row + 17] * ow)
    o_ref[0] = jnp.concatenate(rows, axis=0).T


def _scomb(oc, os_, ow, misc_pad, t_new):
    b = oc.shape[0]
    grp = pl.BlockSpec((1, NSA_KV_HEADS, NSA_GROUP * 8, LANES), lambda i: (i, 0, 0, 0))
    return pl.pallas_call(
        functools.partial(_scomb_kernel, t_new),
        grid=(b,),
        in_specs=[grp, pl.BlockSpec((1, t_new, NSA_KV_HEADS, 8, LANES), lambda i: (i, 0, 0, 0, 0)), grp,
                  pl.BlockSpec((1, 8, LANES), lambda i: (i, 0, 0))],
        out_specs=pl.BlockSpec((1, 8, NSA_HEADS * HEAD_DIM), lambda i: (i, 0, 0)),
        out_shape=jax.ShapeDtypeStruct((b, 8, NSA_HEADS * HEAD_DIM), F32),
        compiler_params=_params(1),
        name="snsa_combine",
    )(oc, os_, ow, misc_pad)


def _col_to_row(col):
    n = col.shape[0]
    return jnp.broadcast_to(col, (n, LANES)).T[0:1, 0:n]


def _smla_kernel(t_new, pt_ref, *refs):
    pages = refs[:PAGES_PER_STEP]
    (qf_ref, wuka_ref, wukb_ref, wuv_ref, new_ref, o_ref,
     qa_ref, qb_ref, m_ref, l_ref, acc_ref) = refs[PAGES_PER_STEP:]
    del pt_ref
    step = pl.program_id(1)
    n_steps = pl.num_programs(1)
    scale = (QK_NOPE + QK_ROPE) ** -0.5
    rows = MLA_HEADS * SMLA_SLOTS
    tail = MLA_ROW - LANES

    @pl.when(step == 0)
    def _():
        for h in range(MLA_HEADS):
            hs = slice(h * SMLA_SLOTS, (h + 1) * SMLA_SLOTS)
            qh = qf_ref[0, :, h * LANES:(h + 1) * LANES].astype(BF16)
            qa_ref[hs, :] = jnp.dot(qh, wuka_ref[h], preferred_element_type=F32)
            qb_ref[hs, :] = jnp.dot(qh, wukb_ref[h], preferred_element_type=F32)
        m_ref[...] = jnp.full(m_ref.shape, NEG, F32)
        l_ref[...] = jnp.zeros(l_ref.shape, F32)
        acc_ref[...] = jnp.zeros(acc_ref.shape, F32)

    qa = qa_ref[...].astype(BF16)
    qb = qb_ref[...].astype(BF16)

    def absorb(x_t, mask):
        s = (jnp.dot(qa, x_t[0:KV_LORA], preferred_element_type=F32)
             + jnp.dot(qb, x_t[tail:MLA_ROW], preferred_element_type=F32)) * scale
        m_old = m_ref[...]
        s_m = s if mask is None else jnp.where(mask, s, NEG)
        m_new = jnp.maximum(m_old, jnp.max(s_m, axis=1, keepdims=True))
        p = jnp.exp(s - m_new)
        if mask is not None:
            p = jnp.where(mask, p, 0.0)
        alpha = jnp.exp(m_old - m_new)
        l_ref[...] = alpha * l_ref[...] + jnp.sum(p, axis=1, keepdims=True)
        acc_ref[...] = (acc_ref[...] * _col_to_row(alpha)
                        + jnp.dot(x_t, p.T.astype(BF16), preferred_element_type=F32))
        m_ref[...] = m_new

    absorb(jnp.concatenate([pages[j][0, 0].astype(BF16) for j in range(PAGES_PER_STEP)], axis=1), None)

    @pl.when(step == n_steps - 1)
    def _():
        c = lax.broadcasted_iota(jnp.int32, (rows, LANES), 1)
        t_row = lax.broadcasted_iota(jnp.int32, (rows, LANES), 0) & (SMLA_SLOTS - 1)
        absorb(new_ref[0].astype(BF16), (c < t_new) & (c <= t_row))
        o_lat = (acc_ref[0:KV_LORA, :] * _col_to_row(1.0 / l_ref[...])).T.astype(BF16)
        outs = [jnp.dot(o_lat[h * SMLA_SLOTS:(h + 1) * SMLA_SLOTS], wuv_ref[h], preferred_element_type=F32)
                for h in range(MLA_HEADS)]
        o_ref[0] = jnp.concatenate([jnp.concatenate(outs[2 * j:2 * j + 2], axis=1)
                                    for j in range(MLA_HEADS // 2)], axis=1)


def _smla(layer, pt, cache_t, qf_pad, wuka, wukb, wuv, new_t, t_new):
    b, n_pages = pt.shape
    page = lambda j: pl.BlockSpec(
        (1, 1, MLA_ROW, LANES), lambda bi, s, pt_ref: (pt_ref[bi, s * PAGES_PER_STEP + j], layer, 0, 0))
    const = lambda shape: pl.BlockSpec(shape, lambda bi, s, pt_ref: (0,) * len(shape))
    rows = MLA_HEADS * SMLA_SLOTS
    grid_spec = pltpu.PrefetchScalarGridSpec(
        num_scalar_prefetch=1,
        grid=(b, n_pages // PAGES_PER_STEP),
        in_specs=[page(j) for j in range(PAGES_PER_STEP)] + [
            pl.BlockSpec((1, SMLA_SLOTS, MLA_HEADS * LANES), lambda bi, s, pt_ref: (bi, 0, 0)),
            const(wuka.shape), const(wukb.shape), const(wuv.shape),
            pl.BlockSpec((1, MLA_ROW, LANES), lambda bi, s, pt_ref: (bi, 0, 0))],
        out_specs=pl.BlockSpec((1, SMLA_SLOTS, MLA_HEADS * V_DIM), lambda bi, s, pt_ref: (bi, 0, 0)),
        scratch_shapes=[pltpu.VMEM((rows, KV_LORA), F32), pltpu.VMEM((rows, LANES), F32),
                        pltpu.VMEM((rows, 1), F32), pltpu.VMEM((rows, 1), F32),
                        pltpu.VMEM((MLA_ROW, rows), F32)],
    )
    return pl.pallas_call(
        functools.partial(_smla_kernel, t_new),
        grid_spec=grid_spec,
        out_shape=jax.ShapeDtypeStruct((b, SMLA_SLOTS, MLA_HEADS * V_DIM), F32),
        compiler_params=_params(2),
        name="smla",
    )(pt, *([cache_t] * PAGES_PER_STEP), qf_pad, wuka, wukb, wuv, new_t)


def _sdil_kernel(t_new, q_ref, new_ref, c0_ref, c1_ref, c2_ref, o_ref):
    hw = DIL_HEADS * HEAD_DIM
    rows = DIL_HEADS * 8
    r_idx = lax.broadcasted_iota(jnp.int32, (rows, hw), 0)
    l_idx = lax.broadcasted_iota(jnp.int32, (rows, hw), 1)
    head_mask = jnp.where((r_idx >> 3) == (l_idx >> 6), 1.0, 0.0)
    outs, lses = [], []
    for g, (c_ref, (window, dil)) in enumerate(zip((c0_ref, c1_ref, c2_ref), DIL_PATTERNS)):
        lb = c_ref.shape[3]
        q = q_ref[0, :, g * hw:(g + 1) * hw]
        qs = jnp.concatenate([q] * DIL_HEADS, axis=0) * head_mask
        kc, vc = c_ref[0, 0, 0:hw, :], c_ref[0, 0, hw:2 * hw, :]
        kn = new_ref[0, g * 2 * hw:g * 2 * hw + hw, :]
        vn = new_ref[0, g * 2 * hw + hw:(g + 1) * 2 * hw, :]
        t_c = lax.broadcasted_iota(jnp.int32, (rows, lb), 0) & 7
        d_c = lb + t_c - lax.broadcasted_iota(jnp.int32, (rows, lb), 1)
        mask_c = (d_c >= 0) & ((d_c & (dil - 1)) == 0) & (d_c <= window)
        t_n = lax.broadcasted_iota(jnp.int32, (rows, LANES), 0) & 7
        c_n = lax.broadcasted_iota(jnp.int32, (rows, LANES), 1)
        d_n = t_n - c_n
        mask_n = (c_n < t_new) & (d_n >= 0) & ((d_n & (dil - 1)) == 0) & (d_n <= window)
        (p1, p2), den, m = _softmax_rows([(_dot(qs, kc), mask_c), (_dot(qs, kn), mask_n)])
        o = (_dot_nt(p1, vc) + _dot_nt(p2, vn)) * head_mask
        outs.append(jnp.sum(o.reshape(DIL_HEADS, 8, hw), axis=0))
        lse = (jnp.log(den) + m) * head_mask
        lses.append(jnp.sum(lse.reshape(DIL_HEADS, 8, hw), axis=0))
    mx = jnp.maximum(jnp.maximum(lses[0], lses[1]), lses[2])
    ws = [jnp.exp(x - mx) for x in lses]
    o_ref[0] = (ws[0] * outs[0] + ws[1] * outs[1] + ws[2] * outs[2]) / (ws[0] + ws[1] + ws[2])


def _sdil(layer, q_pad, new_pad, caches, t_new):
    b = q_pad.shape[0]
    hw = DIL_HEADS * HEAD_DIM
    cspec = lambda c: pl.BlockSpec((1, 1, 2 * hw, c.shape[3]), lambda i: (layer, i, 0, 0))
    return pl.pallas_call(
        functools.partial(_sdil_kernel, t_new),
        grid=(b,),
        in_specs=[pl.BlockSpec((1, 8, N_DIL * hw), lambda i: (i, 0, 0)),
                  pl.BlockSpec((1, N_DIL * 2 * hw, LANES), lambda i: (i, 0, 0))] + [cspec(c) for c in caches],
        out_specs=pl.BlockSpec((1, 8, hw), lambda i: (i, 0, 0)),
        out_shape=jax.ShapeDtypeStruct((b, 8, hw), F32),
        compiler_params=_params(1),
        name="sdil",
    )(q_pad, new_pad, *caches)


def _pack_w_ab(w):
    q_a, kv_a, g_a, q_lat, kv_lat, kpe = jnp.split(w, np.cumsum(AB_SPLITS)[:-1].tolist(), axis=1)
    d = w.shape[0]
    qh = q_a.reshape(d, NSA_HEADS, HEAD_DIM)
    zero = jnp.zeros((d, HEAD_DIM), w.dtype)
    chunks = []
    for h in range(NSA_HEADS):
        pair = [qh[:, h], zero] if h // NSA_GROUP == 0 else [zero, qh[:, h]]
        chunks.append(jnp.concatenate(pair, axis=1))
    misc = jnp.concatenate([kpe, g_a, jnp.zeros((d, LANES - QK_ROPE - 3 * NSA_HEADS), w.dtype)], axis=1)
    return jnp.concatenate(chunks + [kv_a, q_lat, kv_lat, misc], axis=1).astype(BF16)


def _pack_w_uq(w):
    wh = w.reshape(Q_LORA, MLA_HEADS, QK_NOPE + QK_ROPE)
    pad = jnp.zeros((Q_LORA, MLA_HEADS, LANES - QK_NOPE - QK_ROPE), w.dtype)
    return jnp.concatenate([wh[..., QK_NOPE:], pad, wh[..., :QK_NOPE]], axis=-1).reshape(
        Q_LORA, MLA_HEADS * LANES).astype(BF16)


def _pack_w_kvup(w_uk, w_uv):
    k_top = jnp.concatenate([jnp.zeros((KV_LORA, MLA_HEADS, LANES - QK_NOPE), w_uk.dtype), w_uk], axis=-1)
    eye = jnp.eye(LANES, dtype=w_uk.dtype) * (np.arange(LANES) < QK_ROPE)[:, None]
    k_bot = jnp.broadcast_to(eye[:, None, :], (LANES, MLA_HEADS, LANES))
    k_part = jnp.concatenate([k_top, k_bot], axis=0).reshape(KV_LORA + LANES, MLA_HEADS * LANES)
    v_part = jnp.concatenate([w_uv.reshape(KV_LORA, MLA_HEADS * V_DIM),
                              jnp.zeros((LANES, MLA_HEADS * V_DIM), w_uv.dtype)], axis=0)
    return jnp.concatenate([k_part, v_part], axis=1).astype(BF16)


def _pack_w_absorb(w_uk, w_uv):
    zero = jnp.zeros((MLA_HEADS, LANES - QK_NOPE, KV_LORA), w_uk.dtype)
    up_a = jnp.concatenate([zero, jnp.transpose(w_uk, (1, 2, 0))], axis=1)
    place = np.zeros((LANES, LANES), np.float32)
    place[np.arange(QK_ROPE), LANES - QK_ROPE + np.arange(QK_ROPE)] = 1.0
    up_b = jnp.broadcast_to(jnp.asarray(place)[None], (MLA_HEADS, LANES, LANES))
    down = jnp.transpose(w_uv, (1, 0, 2))
    return up_a.astype(BF16), up_b.astype(BF16), down.astype(BF16)


def _block_diag2(phi):
    z = jnp.zeros((HEAD_DIM, HEAD_DIM), phi.dtype)
    return jnp.concatenate([jnp.concatenate([phi[0], z], axis=1),
                            jnp.concatenate([z, phi[1]], axis=1)], axis=0).astype(BF16)


def _pad_rows(x, rows):
    return jnp.pad(x, ((0, 0), (0, rows - x.shape[1]), (0, 0)))


def _stack_group_rows(q, b, t):
    q5 = q.reshape(b, t, NSA_KV_HEADS, NSA_GROUP, LANES)
    q5 = jnp.pad(jnp.transpose(q5, (0, 2, 3, 1, 4)), ((0, 0), (0, 0), (0, 0), (0, 8 - t), (0, 0)))
    return q5.reshape(b, NSA_KV_HEADS, NSA_GROUP * 8, LANES)


def kernel(x_prompt, x_sample, cache_nsa_kv, cache_mla, cache_nsa_win, cache_dil0, cache_dil1, cache_dil2,
           page_table, p_prompt, p_sample, w_in_ab, w_out_ab, nsa_pe_k, nsa_pe_v, nsa_phi_k, nsa_phi_v,
           mla_q_norm, mla_kv_norm, mla_w_uq, mla_w_uk, mla_w_uv, w_in_c, w_out_c, norm_mix, norm_ffn,
           w_gate_up, w_down, norm_ple, w_ple_gate, w_ple_proj, norm_final):
    bp, tp, _ = x_prompt.shape
    bs, ts, _ = x_sample.shape
    depth = norm_mix.shape[0]
    n_ab = w_in_ab.shape[0]
    n_pool, page_size = cache_nsa_kv.shape[:2]
    n_pages = page_table.shape[1]
    past_len = n_pages * page_size
    assert page_size == LANES and tp % (2 * SEL_BLOCK) == 0 and ts <= 8 and past_len % SEL_BLOCK == 0
    assert n_pages % PAGES_PER_STEP == 0 and PAGES_PER_STEP % 4 == 0
    np_, ns_ = bp * tp, bs * ts
    hw = DIL_HEADS * HEAD_DIM

    pos_p = jnp.arange(tp, dtype=jnp.int32)
    pos_s = past_len + jnp.arange(ns_, dtype=jnp.int32) % ts
    tile_p = min(ROW_TILE, tp)
    tile_s = ns_
    tabs = {
        "p": (_rope_tables(pos_p, ROT_DIM, HEAD_DIM, 0), _rope_tables(pos_p, QK_ROPE, LANES, 0), tile_p),
        "s": (_rope_tables(pos_s, ROT_DIM, HEAD_DIM, 0), _rope_tables(pos_s, QK_ROPE, LANES, 0), tile_s),
    }
    expand = jnp.asarray(np.arange(tp)[:, None] // SEL_BLOCK == np.arange(tp // SEL_BLOCK)[None], BF16)

    def rows_last(c, feat):
        perm = tuple(i for i in range(c.ndim) if i != feat) + (feat,)
        return jnp.transpose(c, perm)

    cache_nsa_t = rows_last(cache_nsa_kv, 1).reshape(n_pool, n_ab, 512, page_size)
    cache_mla_t = rows_last(cache_mla, 1)
    cache_win_t = rows_last(cache_nsa_win, 2).reshape(n_ab, bs, 256, -1)
    caches_dil_t = [rows_last(c, 2).reshape(c.shape[0], bs, 2 * hw, -1) for c in (cache_dil0, cache_dil1, cache_dil2)]
    slots_last = lambda x: jnp.transpose(x, (0, 2, 1))

    row2 = lambda v: v.reshape(1, -1)
    h_p = x_prompt.reshape(np_, D_MODEL)
    h_s = x_sample.reshape(ns_, D_MODEL)
    nsa_p, nsa_s, mla_p, mla_s, win_p, win_s = [], [], [], [], [], []
    dil_p = [[] for _ in DIL_PATTERNS]
    dil_s = [[] for _ in DIL_PATTERNS]
    y_p = y_s = None

    for i in range(depth):
        l = i // 2
        final = i == depth - 1
        gm = row2(norm_mix[i])
        if i % 2 == 0:
            w_ab = _pack_w_ab(w_in_ab[l])
            w_uq = _pack_w_uq(mla_w_uq[l])
            gq, gkv = row2(mla_q_norm[l]), row2(mla_kv_norm[l])
            pe = jnp.concatenate([nsa_pe_k[l].reshape(CMP_BLOCK, 128), nsa_pe_v[l].reshape(CMP_BLOCK, 128)], axis=1)
            phik, phiv = _block_diag2(nsa_phi_k[l]), _block_diag2(nsa_phi_v[l])
            w_out = w_out_ab[l].astype(BF16)

            t64, tm_, tile = tabs["p"]
            qc, qr, rows, win, qf, mla, misc = _pre_ab(h_p, gm, w_ab, gq, gkv, w_uq, t64, tm_, tile)
            rows3 = rows.reshape(bp, tp, 512)
            kc, vc = _cmp_prompt(rows3, pe, phik, phiv)
            o_a = _nsa_prompt(qc.reshape(bp, tp, -1), qr.reshape(bp, tp, -1), rows3, win.reshape(bp, tp, 256),
                              kc, vc, misc.reshape(bp, tp, LANES), expand)
            kf, vv = _mla_kv(mla, misc, _pack_w_kvup(mla_w_uk[l], mla_w_uv[l]), tile)
            o_b = _mla_prompt(qf.reshape(bp, tp, -1), kf.reshape(bp, tp, -1), vv.reshape(bp, tp, -1))
            mix_p = jnp.concatenate([o_a.reshape(np_, -1), o_b.reshape(np_, -1)], axis=1)
            nsa_p.append(rows3)
            mla_p.append(mla.reshape(bp, tp, MLA_ROW))
            win_p.append(win.reshape(bp, tp, 256)[:, max(tp - NSA_WINDOW, 0):])

            t64, tm_, tile = tabs["s"]
            qc, qr, rows, win, qf, mla, misc = _pre_ab(h_s, gm, w_ab, gq, gkv, w_uq, t64, tm_, tile)
            o_cmp, sel_idx = _snsa_a(l, page_table, cache_nsa_t, _stack_group_rows(qc, bs, ts),
                                     pe, phik, phiv, ts)
            sidx = jnp.transpose(sel_idx[:, :, :ts, :N_SEL], (0, 2, 1, 3)).reshape(-1)
            q_tg = jnp.pad(qr.reshape(bs, ts, NSA_KV_HEADS, NSA_GROUP, LANES),
                           ((0, 0), (0, 0), (0, 0), (0, 8 - NSA_GROUP), (0, 0)))
            new_blk = slots_last(_pad_rows(rows.reshape(bs, ts, 512)[:, :, 256:512], LANES))
            o_slc = _snsa_b(l, page_table, sidx, cache_nsa_t, q_tg, new_blk, ts)
            o_win = _snsa_win(l, _stack_group_rows(qr, bs, ts), cache_win_t,
                              slots_last(_pad_rows(win.reshape(bs, ts, 256), LANES)), past_len, ts)
            o_a = _scomb(o_cmp, o_slc, o_win, _pad_rows(misc.reshape(bs, ts, LANES), 8), ts)
            o_a = o_a[:, :ts].reshape(ns_, NSA_HEADS * HEAD_DIM)
            mla_new = slots_last(_pad_rows(mla.reshape(bs, ts, MLA_ROW), LANES))
            o_b = _smla(l, page_table, cache_mla_t, _pad_rows(qf.reshape(bs, ts, -1), SMLA_SLOTS),
                        *_pack_w_absorb(mla_w_uk[l], mla_w_uv[l]), mla_new, ts)
            mix_s = jnp.concatenate([o_a, o_b[:, :ts].reshape(ns_, -1)], axis=1)
            nsa_s.append(rows.reshape(bs, ts, 512))
            mla_s.append(mla.reshape(bs, ts, MLA_ROW))
            win_s.append(win.reshape(bs, ts, 256))
        else:
            w_c = w_in_c[l].astype(BF16)
            w_out = w_out_c[l].astype(BF16)

            t64, _, tile = tabs["p"]
            outs = _pre_c(h_p, gm, w_c, t64, tile)
            os_, ls_ = [], []
            for g, (window, dil) in enumerate(DIL_PATTERNS):
                o, lse = _dil_prompt(outs[g], outs[N_DIL + g], bp, tp, dil)
                os_.append(o)
                ls_.append(lse)
                dil_p[g].append(outs[N_DIL + g].reshape(bp, tp, 2 * hw)[:, max(tp - window, 0):])
            mix_p = _dil_merge(os_, ls_, tile)

            t64, _, tile = tabs["s"]
            outs = _pre_c(h_s, gm, w_c, t64, tile)
            q_pad = _pad_rows(jnp.concatenate(outs[:N_DIL], axis=1).reshape(bs, ts, -1), 8)
            new_pad = slots_last(_pad_rows(jnp.concatenate(outs[N_DIL:], axis=1).reshape(bs, ts, -1), LANES))
            mix_s = _sdil(l, q_pad, new_pad, caches_dil_t, ts)[:, :ts].reshape(ns_, hw)
            for g in range(N_DIL):
                dil_s[g].append(outs[N_DIL + g].reshape(bs, ts, 2 * hw))

        post_w = (w_out, row2(norm_ffn[i]), w_gate_up[i].astype(BF16), w_down[i].astype(BF16),
                  row2(norm_ple[i]), w_ple_gate[i].astype(BF16))
        wpp, gfin = w_ple_proj[i].astype(BF16), row2(norm_final)
        res = _post(h_p, mix_p, *post_w, p_prompt[i].reshape(np_, PLE_DIM), wpp, gfin, final, tabs["p"][2])
        h_p = res[0]
        if final:
            y_p = res[1]
        res = _post(h_s, mix_s, *post_w, p_sample[i].reshape(ns_, PLE_DIM), wpp, gfin, final, tabs["s"][2])
        h_s = res[0]
        if final:
            y_s = res[1]

    def rows_out(parts, b, t, tail):
        return jnp.stack(parts, axis=2).reshape((b, t, len(parts)) + tail)

    def bufs_out(parts, tail):
        x = jnp.stack(parts, axis=0)
        return x.reshape(x.shape[:3] + tail)

    kv_tail = (4, NSA_KV_HEADS, HEAD_DIM)
    win_tail = (2, NSA_KV_HEADS, HEAD_DIM)
    dil_tail = (2, DIL_HEADS, HEAD_DIM)
    out = [y_p.reshape(bp, tp, D_MODEL), y_s.reshape(bs, ts, D_MODEL),
           rows_out(nsa_p, bp, tp, kv_tail), rows_out(nsa_s, bs, ts, kv_tail),
           rows_out(mla_p, bp, tp, (MLA_ROW,)), rows_out(mla_s, bs, ts, (MLA_ROW,)),
           bufs_out(win_p, win_tail), bufs_out(win_s, win_tail)]
    for g in range(N_DIL):
        out += [bufs_out(dil_p[g], dil_tail), bufs_out(dil_s[g], dil_tail)]
    return tuple(out)
```

```python
import functools

import numpy as np
import jax
import jax.numpy as jnp
from jax import lax
from jax.experimental import pallas as pl
from jax.experimental.pallas import tpu as pltpu

F32 = jnp.float32
BF16 = jnp.bfloat16

D_MODEL = 1024
HEAD_DIM = 64
ROT_DIM = HEAD_DIM // 4
ROPE_THETA = 500000.0
RMS_EPS = 1e-6
NSA_HEADS = 8
NSA_KV_HEADS = 2
NSA_GROUP = NSA_HEADS // NSA_KV_HEADS
CMP_BLOCK = 32
SEL_BLOCK = 64
N_SEL = 16
NSA_WINDOW = 512
MLA_HEADS = 8
Q_LORA = 384
KV_LORA = 256
QK_NOPE = 64
QK_ROPE = 32
V_DIM = 64
MLA_ROW = KV_LORA + QK_ROPE
DIL_PATTERNS = ((128, 1), (512, 4), (2048, 16))
N_DIL = len(DIL_PATTERNS)
DIL_HEADS = 8
DIL_BACK = 128
D_FF = ((-(-8 * D_MODEL // 3) + 255) // 256) * 256
PLE_DIM = 256
AB_SPLITS = (NSA_HEADS * HEAD_DIM, 6 * NSA_KV_HEADS * HEAD_DIM, 3 * NSA_HEADS, Q_LORA, KV_LORA, QK_ROPE)

LANES = 128
NEG = -1e30
V7X_VMEM_LIMIT = 56 * 1024 * 1024
ROW_TILE = 256
FF_CHUNK = 256
NSA_TQ = 256
NSA_TK = 512
MLA_TQ = 512
MLA_TK = 512
PAGES_PER_STEP = 32
SMLA_SLOTS = 16


def _dot(a, b):
    return jnp.dot(a.astype(BF16), b.astype(BF16), preferred_element_type=F32)


def _dot_nt(a, b):
    return lax.dot_general(a.astype(BF16), b.astype(BF16), (((1,), (1,)), ((), ())),
                           preferred_element_type=F32)


def _rms(x, g):
    return x * lax.rsqrt(jnp.mean(x * x, axis=-1, keepdims=True) + RMS_EPS) * g


def _tile_lanes(a, reps):
    return a if reps == 1 else jnp.concatenate([a] * reps, axis=1)


def _rope(x, cos, sin_up, sin_dn, half):
    w = x.shape[-1]
    reps = w // LANES
    return (x * _tile_lanes(cos, reps)
            + pltpu.roll(x, half, 1) * _tile_lanes(sin_up, reps)
            + pltpu.roll(x, w - half, 1) * _tile_lanes(sin_dn, reps))


def _rope_tables(pos, rot_dim, period, offset):
    half = rot_dim // 2
    inv = jnp.power(ROPE_THETA, -jnp.arange(half, dtype=F32) / half)
    ang = pos.astype(F32)[:, None] * inv[None]
    cos, sin = jnp.cos(ang), jnp.sin(ang)
    lane = np.arange(LANES) % period - offset
    lo = (lane >= 0) & (lane < half)
    hi = (lane >= half) & (lane < rot_dim)
    j = np.clip(np.where(hi, lane - half, lane), 0, half - 1)
    cos_t = jnp.where(lo | hi, cos[:, j], 1.0)
    sin_up = jnp.where(hi, sin[:, j], 0.0)
    sin_dn = jnp.where(lo, -sin[:, j], 0.0)
    return cos_t, sin_up, sin_dn


def _softmax_cols(pieces):
    m = None
    for s, mask in pieces:
        mm = jnp.max(jnp.where(mask, s, NEG), axis=0, keepdims=True)
        m = mm if m is None else jnp.maximum(m, mm)
    es = [jnp.where(mask, jnp.exp(s - m), 0.0) for s, mask in pieces]
    den = sum(jnp.sum(e, axis=0, keepdims=True) for e in es)
    inv = 1.0 / jnp.maximum(den, 1e-30)
    return [e * inv for e in es], den, m


def _softmax_rows(pieces):
    m = None
    for s, mask in pieces:
        mm = jnp.max(jnp.where(mask, s, NEG), axis=1, keepdims=True)
        m = mm if m is None else jnp.maximum(m, mm)
    es = [jnp.where(mask, jnp.exp(s - m), 0.0) for s, mask in pieces]
    den = sum(jnp.sum(e, axis=1, keepdims=True) for e in es)
    inv = 1.0 / jnp.maximum(den, 1e-30)
    return [e * inv for e in es], den, m


def _flash_update(carry, s_t, mask, v_t):
    m, l, acc = carry
    if mask is not None:
        s_t = jnp.where(mask, s_t, NEG)
    m_new = jnp.maximum(m, jnp.max(s_t, axis=0, keepdims=True))
    p = jnp.exp(s_t - m_new)
    alpha = jnp.exp(m - m_new)
    l_new = alpha * l + jnp.sum(p, axis=0, keepdims=True)
    acc_new = alpha * acc + jnp.dot(v_t, p.astype(BF16), preferred_element_type=F32)
    return m_new, l_new, acc_new


def _params(n_grid):
    return pltpu.CompilerParams(dimension_semantics=("arbitrary",) * n_grid,
                                vmem_limit_bytes=V7X_VMEM_LIMIT)


def _resident(shape):
    nd = len(shape)
    return pl.BlockSpec(shape, lambda *_: (0,) * nd, pipeline_mode=pl.Buffered(1))


def _rows(width, tile):
    return pl.BlockSpec((tile, width), lambda i: (i, 0))


def _pages_per_step(n_pages):
    pps = min(PAGES_PER_STEP, n_pages)
    assert n_pages % pps == 0 and pps % 4 == 0
    return pps


def _pre_ab_kernel(h_ref, gm_ref, w_ref, gq_ref, gkv_ref, wuq_ref,
                   c64_ref, u64_ref, d64_ref, cm_ref, um_ref, dm_ref,
                   qc_ref, qr_ref, rows_ref, win_ref, qf_ref, mla_ref, misc_ref):
    t64 = (c64_ref[...], u64_ref[...], d64_ref[...])
    tm_ = (cm_ref[...], um_ref[...], dm_ref[...])
    hn = _rms(h_ref[...], gm_ref[...])
    z = _dot(hn, w_ref[...])
    nq = NSA_HEADS * LANES
    qc = z[:, 0:nq] * (HEAD_DIM ** -0.5)
    qc_ref[...] = qc
    qr_ref[...] = _rope(qc, *t64, ROT_DIM // 2)
    kv = z[:, nq:nq + 768]
    rows_ref[:, 0:256] = kv[:, 0:256]
    rows_ref[:, 256:384] = _rope(kv[:, 256:384], *t64, ROT_DIM // 2)
    rows_ref[:, 384:512] = kv[:, 384:512]
    win_ref[:, 0:128] = _rope(kv[:, 512:640], *t64, ROT_DIM // 2)
    win_ref[:, 128:256] = kv[:, 640:768]
    o = nq + 768
    qn = _rms(z[:, o:o + Q_LORA], gq_ref[...])
    qf_ref[...] = _rope(_dot(qn, wuq_ref[...]), *tm_, QK_ROPE // 2)
    o += Q_LORA
    ckv = _rms(z[:, o:o + KV_LORA], gkv_ref[...])
    o += KV_LORA
    misc = _rope(z[:, o:o + LANES], *tm_, QK_ROPE // 2)
    mla_ref[:, 0:KV_LORA] = ckv
    mla_ref[:, KV_LORA:MLA_ROW] = misc[:, 0:QK_ROPE]
    misc_ref[...] = misc


def _pre_ab(h, gm, w, gq, gkv, wuq, tabs64, tabsm, tile):
    n = h.shape[0]
    t_tab = tabs64[0].shape[0]
    nt = t_tab // tile
    tab = pl.BlockSpec((tile, LANES), lambda i: (i % nt, 0))
    widths = (NSA_HEADS * LANES, NSA_HEADS * LANES, 512, 256, MLA_HEADS * LANES, MLA_ROW, LANES)
    return pl.pallas_call(
        _pre_ab_kernel,
        grid=(n // tile,),
        in_specs=[_rows(D_MODEL, tile), _resident(gm.shape), _resident(w.shape), _resident(gq.shape),
                  _resident(gkv.shape), _resident(wuq.shape)] + [tab] * 6,
        out_specs=[_rows(wd, tile) for wd in widths],
        out_shape=[jax.ShapeDtypeStruct((n, wd), F32) for wd in widths],
        compiler_params=_params(1),
        name="pre_ab",
    )(h, gm, w, gq, gkv, wuq, *tabs64, *tabsm)


def _mla_kv_kernel(mla_ref, misc_ref, w_ref, k_ref, v_ref):
    x = jnp.concatenate([mla_ref[:, 0:KV_LORA], misc_ref[...]], axis=1)
    z = _dot(x, w_ref[...])
    k_ref[...] = z[:, 0:MLA_HEADS * LANES]
    v_ref[...] = z[:, MLA_HEADS * LANES:]


def _mla_kv(mla, misc, w, tile):
    n = mla.shape[0]
    return pl.pallas_call(
        _mla_kv_kernel,
        grid=(n // tile,),
        in_specs=[_rows(MLA_ROW, tile), _rows(LANES, tile), _resident(w.shape)],
        out_specs=[_rows(MLA_HEADS * LANES, tile), _rows(MLA_HEADS * V_DIM, tile)],
        out_shape=[jax.ShapeDtypeStruct((n, MLA_HEADS * LANES), F32),
                   jax.ShapeDtypeStruct((n, MLA_HEADS * V_DIM), F32)],
        compiler_params=_params(1),
        name="mla_kv",
    )(mla, misc, w)


def _pre_c_kernel(h_ref, gm_ref, w_ref, c64_ref, u64_ref, d64_ref, *out_refs):
    t64 = (c64_ref[...], u64_ref[...], d64_ref[...])
    hn = _rms(h_ref[...], gm_ref[...])
    z = _dot(hn, w_ref[...])
    hw = DIL_HEADS * HEAD_DIM
    for g in range(N_DIL):
        o = g * 3 * hw
        out_refs[g][...] = _rope(z[:, o:o + hw] * (HEAD_DIM ** -0.5), *t64, ROT_DIM // 2)
        out_refs[N_DIL + g][:, 0:hw] = _rope(z[:, o + hw:o + 2 * hw], *t64, ROT_DIM // 2)
        out_refs[N_DIL + g][:, hw:2 * hw] = z[:, o + 2 * hw:o + 3 * hw]


def _pre_c(h, gm, w, tabs64, tile):
    n = h.shape[0]
    nt = tabs64[0].shape[0] // tile
    tab = pl.BlockSpec((tile, LANES), lambda i: (i % nt, 0))
    hw = DIL_HEADS * HEAD_DIM
    widths = (hw,) * N_DIL + (2 * hw,) * N_DIL
    return pl.pallas_call(
        _pre_c_kernel,
        grid=(n // tile,),
        in_specs=[_rows(D_MODEL, tile), _resident(gm.shape), _resident(w.shape)] + [tab] * 3,
        out_specs=[_rows(wd, tile) for wd in widths],
        out_shape=[jax.ShapeDtypeStruct((n, wd), F32) for wd in widths],
        compiler_params=_params(1),
        name="pre_c",
    )(h, gm, w, *tabs64)


def _post_kernel(final, n_mix, h_ref, *refs):
    a_refs = refs[:n_mix]
    (wout_ref, gffn_ref, wgu_ref, wd_ref, gple_ref, wpg_ref, ple_ref, wpp_ref, gfin_ref, o_ref) = refs[n_mix:n_mix + 10]
    y_ref = refs[n_mix + 10:]
    h1 = h_ref[...]
    off = 0
    for a_ref in a_refs:
        h1 = h1 + _dot(a_ref[...], wout_ref[off:off + a_ref.shape[1], :])
        off += a_ref.shape[1]
    xn = _rms(h1, gffn_ref[...]).astype(BF16)
    acc = jnp.zeros(h1.shape, F32)
    for c in range(D_FF // FF_CHUNK):
        lo = c * FF_CHUNK
        g = jnp.dot(xn, wgu_ref[:, lo:lo + FF_CHUNK], preferred_element_type=F32)
        u = jnp.dot(xn, wgu_ref[:, D_FF + lo:D_FF + lo + FF_CHUNK], preferred_element_type=F32)
        a = g * jax.nn.sigmoid(g) * u
        acc = acc + jnp.dot(a.astype(BF16), wd_ref[lo:lo + FF_CHUNK, :], preferred_element_type=F32)
    h2 = h1 + acc
    gate = jax.nn.sigmoid(_dot(_rms(h2, gple_ref[...]), wpg_ref[...]))
    h3 = h2 + gate * _dot(ple_ref[...], wpp_ref[...])
    o_ref[...] = h3
    if final:
        y_ref[0][...] = _rms(h3, gfin_ref[...])


def _post(h, mix, wout, gffn, wgu, wd, gple, wpg, ple, wpp, gfin, final, tile):
    n = h.shape[0]
    n_out = 2 if final else 1
    return pl.pallas_call(
        functools.partial(_post_kernel, final, len(mix)),
        grid=(n // tile,),
        in_specs=[_rows(D_MODEL, tile)] + [_rows(a.shape[1], tile) for a in mix] + [
                  _resident(wout.shape), _resident(gffn.shape),
                  _resident(wgu.shape), _resident(wd.shape), _resident(gple.shape), _resident(wpg.shape),
                  _rows(PLE_DIM, tile), _resident(wpp.shape), _resident(gfin.shape)],
        out_specs=[_rows(D_MODEL, tile)] * n_out,
        out_shape=[jax.ShapeDtypeStruct((n, D_MODEL), F32)] * n_out,
        compiler_params=_params(1),
        name="post",
    )(h, *mix, wout, gffn, wgu, wd, gple, wpg, ple, wpp, gfin)


def _cmp_kernel(rows_ref, pe_ref, phik_ref, phiv_ref, kc_ref, vc_ref):
    t = rows_ref.shape[1]
    nsb = t // SEL_BLOCK
    x3 = rows_ref[0, :, 0:256].reshape(nsb, SEL_BLOCK, 256)
    pe = pe_ref[...][None]
    even = jnp.mean(x3[:, 0:CMP_BLOCK, :] + pe, axis=1)
    odd = jnp.mean(x3[:, CMP_BLOCK:SEL_BLOCK, :] + pe, axis=1)
    km = jnp.concatenate([even, odd], axis=0)
    kc_ref[0] = _dot(km[:, 0:128], phik_ref[...])
    vc_ref[0] = _dot(km[:, 128:256], phiv_ref[...])


def _cmp_prompt(rows3, pe, phik, phiv):
    b, t, _ = rows3.shape
    ncb = t // CMP_BLOCK
    return pl.pallas_call(
        _cmp_kernel,
        grid=(b,),
        in_specs=[pl.BlockSpec((1, t, 512), lambda i: (i, 0, 0)), _resident(pe.shape),
                  _resident(phik.shape), _resident(phiv.shape)],
        out_specs=[pl.BlockSpec((1, ncb, LANES), lambda i: (i, 0, 0))] * 2,
        out_shape=[jax.ShapeDtypeStruct((b, ncb, LANES), F32)] * 2,
        compiler_params=_params(1),
        name="nsa_cmp",
    )(rows3, pe, phik, phiv)


def _nsa_prompt_kernel(qc_ref, qr_ref, rows_ref, win_ref, kc_ref, vc_ref, misc_ref, exp_ref, o_ref):
    qi = pl.program_id(1)
    tq = qc_ref.shape[1]
    t_len = rows_ref.shape[1]
    nsb = t_len // SEL_BLOCK
    ncb = 2 * nsb
    q0 = qi * tq
    gates_t = jax.nn.sigmoid(misc_ref[0].T)
    kc = kc_ref[0].astype(BF16)
    vc_t = vc_ref[0].T.astype(BF16)
    qpos1 = q0 + lax.broadcasted_iota(jnp.int32, (1, tq), 1)
    qpos4 = _tile_lanes(qpos1, NSA_GROUP)
    out_rows = []
    for g in range(NSA_KV_HEADS):
        heads = range(g * NSA_GROUP, (g + 1) * NSA_GROUP)
        qc_g = jnp.concatenate([qc_ref[0, :, h * LANES:(h + 1) * LANES] for h in heads], axis=0).astype(BF16)
        qr_g = jnp.concatenate([qr_ref[0, :, h * LANES:(h + 1) * LANES] for h in heads], axis=0).astype(BF16)
        s_c = _dot_nt(kc, qc_g)
        n_idx = lax.broadcasted_iota(jnp.int32, (ncb, NSA_GROUP * tq), 0)
        n_orig = jnp.where(n_idx < nsb, 2 * n_idx, 2 * (n_idx - nsb) + 1)
        cmask = (n_orig + 1) * CMP_BLOCK - 1 <= qpos4
        (p_c,), _, _ = _softmax_cols([(s_c, cmask)])
        o_c = jnp.dot(vc_t, p_c.astype(BF16), preferred_element_type=F32)[g * HEAD_DIM:(g + 1) * HEAD_DIM]
        imp = sum(p_c[:, r * tq:(r + 1) * tq] for r in range(NSA_GROUP))
        imp = imp[0:nsb] + imp[nsb:ncb]
        blk = lax.broadcasted_iota(jnp.int32, (nsb, tq), 0)
        forced = (blk == 0) | (blk == (qpos1 >> 6))
        val = jnp.where(forced, jnp.inf, jnp.where(blk * SEL_BLOCK <= qpos1, imp, -jnp.inf))
        rank = jnp.zeros((nsb, tq), F32)
        for i in range(nsb):
            vi = val[i:i + 1, :]
            rank = rank + jnp.where(vi > val, 1.0, 0.0) + jnp.where(vi == val, jnp.where(blk > i, 1.0, 0.0), 0.0)
        sel = jnp.where(rank < min(N_SEL, nsb), jnp.where(val > -jnp.inf, 1.0, 0.0), 0.0).astype(BF16)

        init = (jnp.full((1, NSA_GROUP * tq), NEG, F32), jnp.zeros((1, NSA_GROUP * tq), F32),
                jnp.zeros((HEAD_DIM, NSA_GROUP * tq), F32))

        def slc_body(kt, carry):
            k0 = pl.multiple_of(kt * NSA_TK, NSA_TK)
            k = rows_ref[0, pl.ds(k0, NSA_TK), 256:384]
            v = rows_ref[0, pl.ds(k0, NSA_TK), 384:512]
            s_t = _dot_nt(k, qr_g)
            chosen = jnp.dot(exp_ref[pl.ds(k0, NSA_TK), :], sel, preferred_element_type=F32)
            kpos = k0 + lax.broadcasted_iota(jnp.int32, (NSA_TK, tq), 0)
            mask = _tile_lanes((chosen > 0.5) & (kpos <= qpos1), NSA_GROUP)
            v_t = v.T[g * HEAD_DIM:(g + 1) * HEAD_DIM].astype(BF16)
            return _flash_update(carry, s_t, mask, v_t)

        n_kt = (q0 + tq - 1) // NSA_TK + 1
        _, l_s, acc_s = lax.fori_loop(0, n_kt, slc_body, init)
        o_s = acc_s / l_s

        carry = init
        n_back_tiles = -(-(NSA_WINDOW - 1) // tq)
        for j in range(n_back_tiles + 1):
            kt = qi - j
            k0 = pl.multiple_of(jnp.maximum(kt, 0) * tq, tq)
            k = win_ref[0, pl.ds(k0, tq), 0:128]
            v = win_ref[0, pl.ds(k0, tq), 128:256]
            s_t = _dot_nt(k, qr_g)
            kpos = kt * tq + lax.broadcasted_iota(jnp.int32, (tq, tq), 0)
            mask = _tile_lanes((kpos <= qpos1) & (kpos >= qpos1 - (NSA_WINDOW - 1)) & (kpos >= 0), NSA_GROUP)
            v_t = v.T[g * HEAD_DIM:(g + 1) * HEAD_DIM].astype(BF16)
            carry = _flash_update(carry, s_t, mask, v_t)
        o_w = carry[2] / carry[1]

        for r, h in enumerate(heads):
            sl = slice(r * tq, (r + 1) * tq)
            row = 32 + h
            out_rows.append(gates_t[row:row + 1] * o_c[:, sl]
                            + gates_t[row + 8:row + 9] * o_s[:, sl]
                            + gates_t[row + 16:row + 17] * o_w[:, sl])
    o_ref[0] = jnp.concatenate(out_rows, axis=0).T


def _nsa_prompt(qc3, qr3, rows3, win3, kc, vc, misc3, expand):
    b, t, _ = rows3.shape
    tq = min(NSA_TQ, t)
    ncb = kc.shape[1]
    qspec = pl.BlockSpec((1, tq, NSA_HEADS * LANES), lambda i, j: (i, j, 0))
    return pl.pallas_call(
        _nsa_prompt_kernel,
        grid=(b, t // tq),
        in_specs=[qspec, qspec,
                  pl.BlockSpec((1, t, 512), lambda i, j: (i, 0, 0)),
                  pl.BlockSpec((1, t, 256), lambda i, j: (i, 0, 0)),
                  pl.BlockSpec((1, ncb, LANES), lambda i, j: (i, 0, 0)),
                  pl.BlockSpec((1, ncb, LANES), lambda i, j: (i, 0, 0)),
                  pl.BlockSpec((1, tq, LANES), lambda i, j: (i, j, 0)),
                  _resident(expand.shape)],
        out_specs=pl.BlockSpec((1, tq, NSA_HEADS * HEAD_DIM), lambda i, j: (i, j, 0)),
        out_shape=jax.ShapeDtypeStruct((b, t, NSA_HEADS * HEAD_DIM), F32),
        compiler_params=_params(2),
        name="nsa_prompt",
    )(qc3, qr3, rows3, win3, kc, vc, misc3, expand)


def _mla_prompt_kernel(q_ref, k_ref, v_ref, o_ref):
    qi = pl.program_id(2)
    tq = q_ref.shape[1]
    tk = min(MLA_TK, k_ref.shape[1])
    q0 = qi * tq
    qpos = q0 + lax.broadcasted_iota(jnp.int32, (1, tq), 1)
    scale = (QK_NOPE + QK_ROPE) ** -0.5
    outs = []
    n_kt = (q0 + tq - 1) // tk + 1
    for hh in range(2):
        q = (q_ref[0, :, hh * LANES:(hh + 1) * LANES] * scale).astype(BF16)

        def tile(kt, carry, causal):
            k0 = pl.multiple_of(kt * tk, tk)
            k = k_ref[0, pl.ds(k0, tk), hh * LANES:(hh + 1) * LANES]
            s_t = _dot_nt(k, q)
            mask = (k0 + lax.broadcasted_iota(jnp.int32, (tk, tq), 0) <= qpos) if causal else None
            v_t = v_ref[0, pl.ds(k0, tk), :].T[hh * V_DIM:(hh + 1) * V_DIM].astype(BF16)
            return _flash_update(carry, s_t, mask, v_t)

        init = (jnp.full((1, tq), NEG, F32), jnp.zeros((1, tq), F32), jnp.zeros((V_DIM, tq), F32))
        carry = lax.fori_loop(0, n_kt - 1, lambda kt, c: tile(kt, c, False), init)
        _, l, acc = tile(n_kt - 1, carry, True)
        outs.append(acc / l)
    o_ref[0] = jnp.concatenate(outs, axis=0).T


def _mla_prompt(qf3, kf3, v3):
    b, t, _ = qf3.shape
    tq = min(MLA_TQ, t)
    return pl.pallas_call(
        _mla_prompt_kernel,
        grid=(b, MLA_HEADS // 2, t // tq),
        in_specs=[pl.BlockSpec((1, tq, 2 * LANES), lambda i, p, j: (i, j, p)),
                  pl.BlockSpec((1, t, 2 * LANES), lambda i, p, j: (i, 0, p)),
                  pl.BlockSpec((1, t, 2 * V_DIM), lambda i, p, j: (i, 0, p))],
        out_specs=pl.BlockSpec((1, tq, 2 * V_DIM), lambda i, p, j: (i, j, p)),
        out_shape=jax.ShapeDtypeStruct((b, t, MLA_HEADS * V_DIM), F32),
        compiler_params=_params(3),
        name="mla_prompt",
    )(qf3, kf3, v3)


def _dil_prompt_kernel(sub, has_prev, q_ref, kc_ref, vc_ref, kp_ref, vp_ref, o_ref, lse_ref):
    i = pl.program_id(2)
    tq = q_ref.shape[1]
    lane = lax.broadcasted_iota(jnp.int32, (1, LANES), 1)
    half_masks = [jnp.where(lane < HEAD_DIM, 1.0, 0.0), jnp.where(lane >= HEAD_DIM, 1.0, 0.0)]
    n_keys = 2 * sub if has_prev else sub
    c_idx = lax.broadcasted_iota(jnp.int32, (n_keys, sub), 0)
    q_idx = lax.broadcasted_iota(jnp.int32, (n_keys, sub), 1)
    kpos = c_idx - sub if has_prev else c_idx
    band = (kpos <= q_idx) & (kpos >= q_idx - DIL_BACK)
    for s in range(tq // sub):
        rs = slice(s * sub, (s + 1) * sub)
        q = q_ref[0, rs, :]
        k = kc_ref[0, rs, :]
        v = vc_ref[0, rs, :]
        mask = band
        if has_prev:
            if s == 0:
                k = jnp.concatenate([kp_ref[0], k], axis=0)
                v = jnp.concatenate([vp_ref[0], v], axis=0)
                mask = band & (c_idx >= jnp.where(i > 0, 0, sub))
            else:
                ps = slice((s - 1) * sub, s * sub)
                k = jnp.concatenate([kc_ref[0, ps, :], k], axis=0)
                v = jnp.concatenate([vc_ref[0, ps, :], v], axis=0)
        o_rows, lse_rows = [], []
        mask2 = _tile_lanes(mask, 2)
        for j in range(DIL_HEADS // 2):
            cs = slice(j * LANES, (j + 1) * LANES)
            v_t = v[:, cs].T.astype(BF16)
            q2 = jnp.concatenate([q[:, cs] * half_masks[0], q[:, cs] * half_masks[1]], axis=0)
            s_t = _dot_nt(k[:, cs], q2)
            (p,), den, m = _softmax_cols([(s_t, mask2)])
            o2 = jnp.dot(v_t, p.astype(BF16), preferred_element_type=F32)
            lse2 = jnp.log(den) + m
            for hh in range(2):
                o_rows.append(o2[hh * HEAD_DIM:(hh + 1) * HEAD_DIM, hh * sub:(hh + 1) * sub])
                lse_rows.append(jnp.broadcast_to(lse2[:, hh * sub:(hh + 1) * sub], (HEAD_DIM, sub)))
        o_ref[0, rs, :] = jnp.concatenate(o_rows, axis=0).T
        lse_ref[0, rs, :] = jnp.concatenate(lse_rows, axis=0).T


def _dil_prompt(q, kv, b, t, dil):
    hw = DIL_HEADS * HEAD_DIM
    l = t // dil
    sub = min(LANES, l)
    has_prev = l > sub
    tq = min(l, 4 * sub)
    per = tq // sub
    q3 = q.reshape(b, l, dil * hw)
    kv3 = kv.reshape(b, l, dil * 2 * hw)
    cur = lambda off: pl.BlockSpec((1, tq, hw), lambda bi, r, i: (bi, i, 2 * r + off))
    prev = lambda off: pl.BlockSpec((1, sub, hw), lambda bi, r, i: (bi, jnp.maximum(i * per - 1, 0), 2 * r + off))
    qo = pl.BlockSpec((1, tq, hw), lambda bi, r, i: (bi, i, r))
    o, lse = pl.pallas_call(
        functools.partial(_dil_prompt_kernel, sub, has_prev),
        grid=(b, dil, l // tq),
        in_specs=[qo, cur(0), cur(1), prev(0), prev(1)],
        out_specs=[qo, qo],
        out_shape=[jax.ShapeDtypeStruct((b, l, dil * hw), F32)] * 2,
        compiler_params=_params(3),
        name="dil_prompt",
    )(q3, kv3, kv3, kv3, kv3)
    return o.reshape(b * t, hw), lse.reshape(b * t, hw)


def _dil_merge_kernel(o0, o1, o2, l0, l1, l2, out_ref):
    ls = [l0[...], l1[...], l2[...]]
    mx = jnp.maximum(jnp.maximum(ls[0], ls[1]), ls[2])
    ws = [jnp.exp(x - mx) for x in ls]
    num = ws[0] * o0[...] + ws[1] * o1[...] + ws[2] * o2[...]
    out_ref[...] = num / (ws[0] + ws[1] + ws[2])


def _dil_merge(outs, lses, tile):
    n, hw = outs[0].shape
    return pl.pallas_call(
        _dil_merge_kernel,
        grid=(n // tile,),
        in_specs=[_rows(hw, tile)] * 6,
        out_specs=_rows(hw, tile),
        out_shape=jax.ShapeDtypeStruct((n, hw), F32),
        compiler_params=_params(1),
        name="dil_merge",
    )(*outs, *lses)


def _snsa_a_kernel(pps, past_len, t_new, pt_ref, *refs):
    pages = refs[:pps]
    qc_ref, pe_ref, phik_ref, phiv_ref, ocmp_ref, idx_ref, cme_ref, cmo_ref = refs[pps:]
    step = pl.program_id(1)
    n_steps = pl.num_programs(1)
    pe = pe_ref[...][None]
    for quad in range(pps // 4):
        x = jnp.concatenate([pages[4 * quad + j][0, 0].T for j in range(4)], axis=0)
        x3 = x.reshape(8, SEL_BLOCK, 256)
        r0 = pl.multiple_of(step * (2 * pps) + quad * 8, 8)
        cme_ref[pl.ds(r0, 8), :] = jnp.mean(x3[:, 0:CMP_BLOCK, :] + pe, axis=1)
        cmo_ref[pl.ds(r0, 8), :] = jnp.mean(x3[:, CMP_BLOCK:SEL_BLOCK, :] + pe, axis=1)

    @pl.when(step == n_steps - 1)
    def _():
        nsb_past = cme_ref.shape[0]
        ncb = 2 * nsb_past
        km = jnp.concatenate([cme_ref[...], cmo_ref[...]], axis=0)
        kc = _dot(km[:, 0:128], phik_ref[...]).astype(BF16)
        vc = _dot(km[:, 128:256], phiv_ref[...]).astype(BF16)
        rows = NSA_GROUP * 8
        t_row = lax.broadcasted_iota(jnp.int32, (rows, 1), 0) & 7
        pos = past_len + t_row
        n_idx = lax.broadcasted_iota(jnp.int32, (rows, ncb), 1)
        n_orig = jnp.where(n_idx < nsb_past, 2 * n_idx, 2 * (n_idx - nsb_past) + 1)
        cmask = (n_orig + 1) * CMP_BLOCK - 1 <= pos
        nsb = -(-(past_len + t_new) // SEL_BLOCK)
        wide = -(-nsb // LANES) * LANES
        blk = lax.broadcasted_iota(jnp.int32, (8, wide), 1)
        blk_f = blk.astype(F32)
        pos8 = past_len + lax.broadcasted_iota(jnp.int32, (8, 1), 0)
        lane = lax.broadcasted_iota(jnp.int32, (8, LANES), 1)
        for g in range(NSA_KV_HEADS):
            s = _dot_nt(qc_ref[0, g], kc)
            (p,), _, _ = _softmax_rows([(s, cmask)])
            ocmp_ref[0, g] = jnp.dot(p.astype(BF16), vc, preferred_element_type=F32)
            imp = sum(p[r * 8:(r + 1) * 8, 0:nsb_past] + p[r * 8:(r + 1) * 8, nsb_past:ncb]
                      for r in range(NSA_GROUP))
            imp = jnp.concatenate([imp, jnp.zeros((8, wide - nsb_past), F32)], axis=1)
            forced = (blk == 0) | (blk == (pos8 >> 6))
            val = jnp.where(forced, jnp.inf, jnp.where(blk * SEL_BLOCK <= pos8, imp, -jnp.inf))
            picks = jnp.full((8, LANES), -1.0, F32)
            for k in range(min(N_SEL, nsb)):
                mx = jnp.max(val, axis=1, keepdims=True)
                first = jnp.min(jnp.where(val == mx, blk_f, 1e9), axis=1, keepdims=True)
                picks = jnp.where(lane == k, jnp.where(mx > -jnp.inf, first, -1.0), picks)
                val = jnp.where(blk_f == first, -jnp.inf, val)
            idx_ref[0, g] = picks.astype(jnp.int32)


def _snsa_a(layer, pt, cache_v, qc_st, pe, phik, phiv, t_new):
    b, n_pages = pt.shape
    past_len = n_pages * LANES
    nsb_past = past_len // SEL_BLOCK
    pps = _pages_per_step(n_pages)
    page = lambda j: pl.BlockSpec(
        (1, 1, 256, LANES), lambda bi, s, pt_ref: (pt_ref[bi, s * pps + j], layer, 0, 0))
    const = lambda shape: pl.BlockSpec(shape, lambda bi, s, pt_ref: (0,) * len(shape))
    grid_spec = pltpu.PrefetchScalarGridSpec(
        num_scalar_prefetch=1,
        grid=(b, n_pages // pps),
        in_specs=[page(j) for j in range(pps)] + [
            pl.BlockSpec((1, NSA_KV_HEADS, NSA_GROUP * 8, LANES), lambda bi, s, pt_ref: (bi, 0, 0, 0)),
            const(pe.shape), const(phik.shape), const(phiv.shape)],
        out_specs=[pl.BlockSpec((1, NSA_KV_HEADS, NSA_GROUP * 8, LANES), lambda bi, s, pt_ref: (bi, 0, 0, 0)),
                   pl.BlockSpec((1, NSA_KV_HEADS, 8, LANES), lambda bi, s, pt_ref: (bi, 0, 0, 0))],
        scratch_shapes=[pltpu.VMEM((nsb_past, 256), F32), pltpu.VMEM((nsb_past, 256), F32)],
    )
    return pl.pallas_call(
        functools.partial(_snsa_a_kernel, pps, past_len, t_new),
        grid_spec=grid_spec,
        out_shape=[jax.ShapeDtypeStruct((b, NSA_KV_HEADS, NSA_GROUP * 8, LANES), F32),
                   jax.ShapeDtypeStruct((b, NSA_KV_HEADS, 8, LANES), jnp.int32)],
        compiler_params=_params(2),
        name="snsa_cmp_topk",
    )(pt, *([cache_v] * pps), qc_st, pe, phik, phiv)


def _snsa_b_kernel(past_len, t_new, n_t, pt_ref, sidx_ref, *refs):
    n_sel = N_SEL
    blocks = refs[:n_sel]
    q_ref, new_ref, o_ref = refs[n_sel:]
    del pt_ref
    bi, ti, gi = pl.program_id(0), pl.program_id(1), pl.program_id(2)
    base = ((bi * n_t + ti) * NSA_KV_HEADS + gi) * n_sel
    new_blk = past_len // SEL_BLOCK
    pos = past_len + ti
    q = q_ref[0, 0, 0].astype(BF16)
    k_new = new_ref[0, 0:128, :]
    v_new = new_ref[0, 128:256, :]
    lane = lax.broadcasted_iota(jnp.int32, (8, LANES), 1)
    pieces, vals = [], []
    for j in range(n_sel):
        idx = sidx_ref[base + j]
        is_new = idx == new_blk
        blk = blocks[j][0, 0]
        k_t = jnp.where(is_new, k_new, blk[0:128]).astype(BF16)
        vals.append(jnp.where(is_new, v_new, blk[128:256]).astype(BF16))
        s = jnp.dot(q, k_t, preferred_element_type=F32)
        idx_v = lane * 0 + idx
        kpos = (idx_v >> 1) * LANES + lane
        in_block = (lane >> 6) == (idx_v & 1)
        pieces.append((s, (idx_v >= 0) & in_block & (kpos <= pos) & (kpos < past_len + t_new)))
    ps, _, _ = _softmax_rows(pieces)
    o_ref[0, 0, 0] = sum(_dot_nt(p, v) for p, v in zip(ps, vals))


def _snsa_b(layer, pt, sidx, cache_t, q_tg, new_blk_t, t_new):
    b, n_pages = pt.shape
    past_len = n_pages * LANES
    n_cache_blk = past_len // SEL_BLOCK

    def blk_spec(j):
        def imap(bi, ti, gi, pt_ref, sidx_ref):
            idx = sidx_ref[((bi * t_new + ti) * NSA_KV_HEADS + gi) * N_SEL + j]
            idx = jnp.clip(idx, 0, n_cache_blk - 1)
            return (pt_ref[bi, idx // 2], layer, 1, 0)
        return pl.BlockSpec((1, 1, 256, LANES), imap)

    grid_spec = pltpu.PrefetchScalarGridSpec(
        num_scalar_prefetch=2,
        grid=(b, t_new, NSA_KV_HEADS),
        in_specs=[blk_spec(j) for j in range(N_SEL)] + [
            pl.BlockSpec((1, 1, 1, 8, LANES), lambda bi, ti, gi, *_: (bi, ti, gi, 0, 0)),
            pl.BlockSpec((1, 256, LANES), lambda bi, ti, gi, *_: (bi, 0, 0))],
        out_specs=pl.BlockSpec((1, 1, 1, 8, LANES), lambda bi, ti, gi, *_: (bi, ti, gi, 0, 0)),
    )
    return pl.pallas_call(
        functools.partial(_snsa_b_kernel, past_len, t_new, t_new),
        grid_spec=grid_spec,
        out_shape=jax.ShapeDtypeStruct((b, t_new, NSA_KV_HEADS, 8, LANES), F32),
        compiler_params=_params(3),
        name="snsa_slc",
    )(pt, sidx, *([cache_t] * N_SEL), q_tg, new_blk_t)


def _snsa_win_kernel(past_len, t_new, q_ref, c_ref, new_ref, o_ref):
    lw = c_ref.shape[3]
    rows = NSA_GROUP * 8
    t_row = lax.broadcasted_iota(jnp.int32, (rows, 1), 0) & 7
    pos = past_len + t_row
    kpos_c = (past_len - lw) + lax.broadcasted_iota(jnp.int32, (rows, lw), 1)
    c_new = lax.broadcasted_iota(jnp.int32, (rows, LANES), 1)
    kpos_n = past_len + c_new
    mask_c = (kpos_c <= pos) & (kpos_c >= pos - (NSA_WINDOW - 1))
    mask_n = (c_new < t_new) & (kpos_n <= pos) & (kpos_n >= pos - (NSA_WINDOW - 1))
    kc, vc = c_ref[0, 0, 0:128, :], c_ref[0, 0, 128:256, :]
    kn, vn = new_ref[0, 0:128, :], new_ref[0, 128:256, :]
    for g in range(NSA_KV_HEADS):
        q = q_ref[0, g]
        (p1, p2), _, _ = _softmax_rows([(_dot(q, kc), mask_c), (_dot(q, kn), mask_n)])
        o_ref[0, g] = _dot_nt(p1, vc) + _dot_nt(p2, vn)


def _snsa_win(layer, qr_st, cache_win, win_new, past_len, t_new):
    b = qr_st.shape[0]
    lw = cache_win.shape[3]
    return pl.pallas_call(
        functools.partial(_snsa_win_kernel, past_len, t_new),
        grid=(b,),
        in_specs=[pl.BlockSpec((1, NSA_KV_HEADS, NSA_GROUP * 8, LANES), lambda i: (i, 0, 0, 0)),
                  pl.BlockSpec((1, 1, 256, lw), lambda i: (layer, i, 0, 0)),
                  pl.BlockSpec((1, 256, LANES), lambda i: (i, 0, 0))],
        out_specs=pl.BlockSpec((1, NSA_KV_HEADS, NSA_GROUP * 8, LANES), lambda i: (i, 0, 0, 0)),
        out_shape=jax.ShapeDtypeStruct((b, NSA_KV_HEADS, NSA_GROUP * 8, LANES), F32),
        compiler_params=_params(1),
        name="snsa_win",
    )(qr_st, cache_win, win_new)


def _scomb_kernel(t_new, oc_ref, os_ref, ow_ref, misc_ref, o_ref):
    gates_t = jax.nn.sigmoid(misc_ref[0].T)
    pad = jnp.zeros((8 - t_new, LANES), F32)
    rows = []
    for g in range(NSA_KV_HEADS):
        gs = slice(g * HEAD_DIM, (g + 1) * HEAD_DIM)
        for r in range(NSA_GROUP):
            h = g * NSA_GROUP + r
            oc = oc_ref[0, g, r * 8:(r + 1) * 8, :].T[gs]
            ow = ow_ref[0, g, r * 8:(r + 1) * 8, :].T[gs]
            os_ = jnp.concatenate([os_ref[0, t, g, r:r + 1, :] for t in range(t_new)] + [pad], axis=0).T[gs]
            row = 32 + h
            rows.append(gates_t[row:row + 1] * oc + gates_t[row + 8:row + 9] * os_ + gates_t[row + 16:row + 17] * ow)
    o_ref[0] = jnp.concatenate(rows, axis=0).T


def _scomb(oc, os_, ow, misc_pad, t_new):
    b = oc.shape[0]
    grp = pl.BlockSpec((1, NSA_KV_HEADS, NSA_GROUP * 8, LANES), lambda i: (i, 0, 0, 0))
    return pl.pallas_call(
        functools.partial(_scomb_kernel, t_new),
        grid=(b,),
        in_specs=[grp, pl.BlockSpec((1, t_new, NSA_KV_HEADS, 8, LANES), lambda i: (i, 0, 0, 0, 0)), grp,
                  pl.BlockSpec((1, 8, LANES), lambda i: (i, 0, 0))],
        out_specs=pl.BlockSpec((1, 8, NSA_HEADS * HEAD_DIM), lambda i: (i, 0, 0)),
        out_shape=jax.ShapeDtypeStruct((b, 8, NSA_HEADS * HEAD_DIM), F32),
        compiler_params=_params(1),
        name="snsa_combine",
    )(oc, os_, ow, misc_pad)


def _col_to_row(col):
    n = col.shape[0]
    return jnp.broadcast_to(col, (n, LANES)).T[0:1, 0:n]


def _smla_kernel(pps, t_new, pt_ref, *refs):
    pages = refs[:pps]
    (qf_ref, wuka_ref, wukb_ref, wuv_ref, new_ref, o_ref,
     qa_ref, qb_ref, m_ref, l_ref, acc_ref) = refs[pps:]
    del pt_ref
    step = pl.program_id(1)
    n_steps = pl.num_programs(1)
    scale = (QK_NOPE + QK_ROPE) ** -0.5
    rows = MLA_HEADS * SMLA_SLOTS
    tail = MLA_ROW - LANES

    @pl.when(step == 0)
    def _():
        for h in range(MLA_HEADS):
            hs = slice(h * SMLA_SLOTS, (h + 1) * SMLA_SLOTS)
            qh = qf_ref[0, :, h * LANES:(h + 1) * LANES].astype(BF16)
            qa_ref[hs, :] = jnp.dot(qh, wuka_ref[h], preferred_element_type=F32)
            qb_ref[hs, :] = jnp.dot(qh, wukb_ref[h], preferred_element_type=F32)
        m_ref[...] = jnp.full(m_ref.shape, NEG, F32)
        l_ref[...] = jnp.zeros(l_ref.shape, F32)
        acc_ref[...] = jnp.zeros(acc_ref.shape, F32)

    qa = qa_ref[...].astype(BF16)
    qb = qb_ref[...].astype(BF16)

    def absorb(x_t, mask):
        s = (jnp.dot(qa, x_t[0:KV_LORA], preferred_element_type=F32)
             + jnp.dot(qb, x_t[tail:MLA_ROW], preferred_element_type=F32)) * scale
        m_old = m_ref[...]
        s_m = s if mask is None else jnp.where(mask, s, NEG)
        m_new = jnp.maximum(m_old, jnp.max(s_m, axis=1, keepdims=True))
        p = jnp.exp(s - m_new)
        if mask is not None:
            p = jnp.where(mask, p, 0.0)
        alpha = jnp.exp(m_old - m_new)
        l_ref[...] = alpha * l_ref[...] + jnp.sum(p, axis=1, keepdims=True)
        acc_ref[...] = (acc_ref[...] * _col_to_row(alpha)
                        + jnp.dot(x_t, p.T.astype(BF16), preferred_element_type=F32))
        m_ref[...] = m_new

    absorb(jnp.concatenate([pages[j][0, 0].astype(BF16) for j in range(pps)], axis=1), None)

    @pl.when(step == n_steps - 1)
    def _():
        c = lax.broadcasted_iota(jnp.int32, (rows, LANES), 1)
        t_row = lax.broadcasted_iota(jnp.int32, (rows, LANES), 0) & (SMLA_SLOTS - 1)
        absorb(new_ref[0].astype(BF16), (c < t_new) & (c <= t_row))
        o_lat = (acc_ref[0:KV_LORA, :] * _col_to_row(1.0 / l_ref[...])).T.astype(BF16)
        outs = [jnp.dot(o_lat[h * SMLA_SLOTS:(h + 1) * SMLA_SLOTS], wuv_ref[h], preferred_element_type=F32)
                for h in range(MLA_HEADS)]
        o_ref[0] = jnp.concatenate([jnp.concatenate(outs[2 * j:2 * j + 2], axis=1)
                                    for j in range(MLA_HEADS // 2)], axis=1)


def _smla(layer, pt, cache_t, qf_pad, wuka, wukb, wuv, new_t, t_new):
    b, n_pages = pt.shape
    pps = _pages_per_step(n_pages)
    page = lambda j: pl.BlockSpec(
        (1, 1, MLA_ROW, LANES), lambda bi, s, pt_ref: (pt_ref[bi, s * pps + j], layer, 0, 0))
    const = lambda shape: pl.BlockSpec(shape, lambda bi, s, pt_ref: (0,) * len(shape))
    rows = MLA_HEADS * SMLA_SLOTS
    grid_spec = pltpu.PrefetchScalarGridSpec(
        num_scalar_prefetch=1,
        grid=(b, n_pages // pps),
        in_specs=[page(j) for j in range(pps)] + [
            pl.BlockSpec((1, SMLA_SLOTS, MLA_HEADS * LANES), lambda bi, s, pt_ref: (bi, 0, 0)),
            const(wuka.shape), const(wukb.shape), const(wuv.shape),
            pl.BlockSpec((1, MLA_ROW, LANES), lambda bi, s, pt_ref: (bi, 0, 0))],
        out_specs=pl.BlockSpec((1, SMLA_SLOTS, MLA_HEADS * V_DIM), lambda bi, s, pt_ref: (bi, 0, 0)),
        scratch_shapes=[pltpu.VMEM((rows, KV_LORA), F32), pltpu.VMEM((rows, LANES), F32),
                        pltpu.VMEM((rows, 1), F32), pltpu.VMEM((rows, 1), F32),
                        pltpu.VMEM((MLA_ROW, rows), F32)],
    )
    return pl.pallas_call(
        functools.partial(_smla_kernel, pps, t_new),
        grid_spec=grid_spec,
        out_shape=jax.ShapeDtypeStruct((b, SMLA_SLOTS, MLA_HEADS * V_DIM), F32),
        compiler_params=_params(2),
        name="smla",
    )(pt, *([cache_t] * pps), qf_pad, wuka, wukb, wuv, new_t)


def _sdil_kernel(t_new, q_ref, new_ref, c0_ref, c1_ref, c2_ref, o_ref):
    hw = DIL_HEADS * HEAD_DIM
    rows = DIL_HEADS * 8
    r_idx = lax.broadcasted_iota(jnp.int32, (rows, hw), 0)
    l_idx = lax.broadcasted_iota(jnp.int32, (rows, hw), 1)
    head_mask = jnp.where((r_idx >> 3) == (l_idx >> 6), 1.0, 0.0)
    outs, lses = [], []
    for g, (c_ref, (window, dil)) in enumerate(zip((c0_ref, c1_ref, c2_ref), DIL_PATTERNS)):
        lb = c_ref.shape[3]
        q = q_ref[0, :, g * hw:(g + 1) * hw]
        qs = jnp.concatenate([q] * DIL_HEADS, axis=0) * head_mask
        kc, vc = c_ref[0, 0, 0:hw, :], c_ref[0, 0, hw:2 * hw, :]
        kn = new_ref[0, g * 2 * hw:g * 2 * hw + hw, :]
        vn = new_ref[0, g * 2 * hw + hw:(g + 1) * 2 * hw, :]
        t_c = lax.broadcasted_iota(jnp.int32, (rows, lb), 0) & 7
        d_c = lb + t_c - lax.broadcasted_iota(jnp.int32, (rows, lb), 1)
        mask_c = (d_c >= 0) & ((d_c & (dil - 1)) == 0) & (d_c <= window)
        t_n = lax.broadcasted_iota(jnp.int32, (rows, LANES), 0) & 7
        c_n = lax.broadcasted_iota(jnp.int32, (rows, LANES), 1)
        d_n = t_n - c_n
        mask_n = (c_n < t_new) & (d_n >= 0) & ((d_n & (dil - 1)) == 0) & (d_n <= window)
        (p1, p2), den, m = _softmax_rows([(_dot(qs, kc), mask_c), (_dot(qs, kn), mask_n)])
        o = (_dot_nt(p1, vc) + _dot_nt(p2, vn)) * head_mask
        outs.append(jnp.sum(o.reshape(DIL_HEADS, 8, hw), axis=0))
        lse = (jnp.log(den) + m) * head_mask
        lses.append(jnp.sum(lse.reshape(DIL_HEADS, 8, hw), axis=0))
    mx = jnp.maximum(jnp.maximum(lses[0], lses[1]), lses[2])
    ws = [jnp.exp(x - mx) for x in lses]
    o_ref[0] = (ws[0] * outs[0] + ws[1] * outs[1] + ws[2] * outs[2]) / (ws[0] + ws[1] + ws[2])


def _sdil(layer, q_pad, new_pad, caches, t_new):
    b = q_pad.shape[0]
    hw = DIL_HEADS * HEAD_DIM
    cspec = lambda c: pl.BlockSpec((1, 1, 2 * hw, c.shape[3]), lambda i: (layer, i, 0, 0))
    return pl.pallas_call(
        functools.partial(_sdil_kernel, t_new),
        grid=(b,),
        in_specs=[pl.BlockSpec((1, 8, N_DIL * hw), lambda i: (i, 0, 0)),
                  pl.BlockSpec((1, N_DIL * 2 * hw, LANES), lambda i: (i, 0, 0))] + [cspec(c) for c in caches],
        out_specs=pl.BlockSpec((1, 8, hw), lambda i: (i, 0, 0)),
        out_shape=jax.ShapeDtypeStruct((b, 8, hw), F32),
        compiler_params=_params(1),
        name="sdil",
    )(q_pad, new_pad, *caches)


def _pack_w_ab(w):
    q_a, kv_a, g_a, q_lat, kv_lat, kpe = jnp.split(w, np.cumsum(AB_SPLITS)[:-1].tolist(), axis=1)
    d = w.shape[0]
    qh = q_a.reshape(d, NSA_HEADS, HEAD_DIM)
    zero = jnp.zeros((d, HEAD_DIM), w.dtype)
    chunks = []
    for h in range(NSA_HEADS):
        pair = [qh[:, h], zero] if h // NSA_GROUP == 0 else [zero, qh[:, h]]
        chunks.append(jnp.concatenate(pair, axis=1))
    misc = jnp.concatenate([kpe, g_a, jnp.zeros((d, LANES - QK_ROPE - 3 * NSA_HEADS), w.dtype)], axis=1)
    return jnp.concatenate(chunks + [kv_a, q_lat, kv_lat, misc], axis=1).astype(BF16)


def _pack_w_uq(w):
    wh = w.reshape(Q_LORA, MLA_HEADS, QK_NOPE + QK_ROPE)
    pad = jnp.zeros((Q_LORA, MLA_HEADS, LANES - QK_NOPE - QK_ROPE), w.dtype)
    return jnp.concatenate([wh[..., QK_NOPE:], pad, wh[..., :QK_NOPE]], axis=-1).reshape(
        Q_LORA, MLA_HEADS * LANES).astype(BF16)


def _pack_w_kvup(w_uk, w_uv):
    k_top = jnp.concatenate([jnp.zeros((KV_LORA, MLA_HEADS, LANES - QK_NOPE), w_uk.dtype), w_uk], axis=-1)
    eye = jnp.eye(LANES, dtype=w_uk.dtype) * (np.arange(LANES) < QK_ROPE)[:, None]
    k_bot = jnp.broadcast_to(eye[:, None, :], (LANES, MLA_HEADS, LANES))
    k_part = jnp.concatenate([k_top, k_bot], axis=0).reshape(KV_LORA + LANES, MLA_HEADS * LANES)
    v_part = jnp.concatenate([w_uv.reshape(KV_LORA, MLA_HEADS * V_DIM),
                              jnp.zeros((LANES, MLA_HEADS * V_DIM), w_uv.dtype)], axis=0)
    return jnp.concatenate([k_part, v_part], axis=1).astype(BF16)


def _pack_w_absorb(w_uk, w_uv):
    zero = jnp.zeros((MLA_HEADS, LANES - QK_NOPE, KV_LORA), w_uk.dtype)
    up_a = jnp.concatenate([zero, jnp.transpose(w_uk, (1, 2, 0))], axis=1)
    place = np.zeros((LANES, LANES), np.float32)
    place[np.arange(QK_ROPE), LANES - QK_ROPE + np.arange(QK_ROPE)] = 1.0
    up_b = jnp.broadcast_to(jnp.asarray(place)[None], (MLA_HEADS, LANES, LANES))
    down = jnp.transpose(w_uv, (1, 0, 2))
    return up_a.astype(BF16), up_b.astype(BF16), down.astype(BF16)


def _block_diag2(phi):
    z = jnp.zeros((HEAD_DIM, HEAD_DIM), phi.dtype)
    return jnp.concatenate([jnp.concatenate([phi[0], z], axis=1),
                            jnp.concatenate([z, phi[1]], axis=1)], axis=0).astype(BF16)


def _pad_rows(x, rows):
    return jnp.pad(x, ((0, 0), (0, rows - x.shape[1]), (0, 0)))


def _stack_group_rows(q, b, t):
    q5 = q.reshape(b, t, NSA_KV_HEADS, NSA_GROUP, LANES)
    q5 = jnp.pad(jnp.transpose(q5, (0, 2, 3, 1, 4)), ((0, 0), (0, 0), (0, 0), (0, 8 - t), (0, 0)))
    return q5.reshape(b, NSA_KV_HEADS, NSA_GROUP * 8, LANES)


def kernel(x_prompt, x_sample, cache_nsa_kv, cache_mla, cache_nsa_win, cache_dil0, cache_dil1, cache_dil2,
           page_table, p_prompt, p_sample, w_in_ab, w_out_ab, nsa_pe_k, nsa_pe_v, nsa_phi_k, nsa_phi_v,
           mla_q_norm, mla_kv_norm, mla_w_uq, mla_w_uk, mla_w_uv, w_in_c, w_out_c, norm_mix, norm_ffn,
           w_gate_up, w_down, norm_ple, w_ple_gate, w_ple_proj, norm_final):
    bp, tp, _ = x_prompt.shape
    bs, ts, _ = x_sample.shape
    depth = norm_mix.shape[0]
    n_ab = w_in_ab.shape[0]
    n_pool, page_size = cache_nsa_kv.shape[:2]
    n_pages = page_table.shape[1]
    past_len = n_pages * page_size
    assert page_size == LANES and tp % (2 * SEL_BLOCK) == 0 and ts <= 8 and past_len % SEL_BLOCK == 0
    assert MLA_TQ == MLA_TK
    np_, ns_ = bp * tp, bs * ts
    hw = DIL_HEADS * HEAD_DIM

    pos_p = jnp.arange(tp, dtype=jnp.int32)
    pos_s = past_len + jnp.arange(ns_, dtype=jnp.int32) % ts
    tile_p = min(ROW_TILE, tp)
    tile_s = ns_
    tabs = {
        "p": (_rope_tables(pos_p, ROT_DIM, HEAD_DIM, 0), _rope_tables(pos_p, QK_ROPE, LANES, 0), tile_p),
        "s": (_rope_tables(pos_s, ROT_DIM, HEAD_DIM, 0), _rope_tables(pos_s, QK_ROPE, LANES, 0), tile_s),
    }
    expand = jnp.asarray(np.arange(tp)[:, None] // SEL_BLOCK == np.arange(tp // SEL_BLOCK)[None], BF16)

    def rows_last(c, feat):
        perm = tuple(i for i in range(c.ndim) if i != feat) + (feat,)
        return jnp.transpose(c, perm)

    cache_nsa_t = rows_last(cache_nsa_kv, 1).reshape(n_pool, n_ab, 512, page_size)
    cache_mla_t = rows_last(cache_mla, 1)
    cache_win_t = rows_last(cache_nsa_win, 2).reshape(n_ab, bs, 256, -1)
    caches_dil_t = [rows_last(c, 2).reshape(c.shape[0], bs, 2 * hw, -1) for c in (cache_dil0, cache_dil1, cache_dil2)]
    slots_last = lambda x: jnp.transpose(x, (0, 2, 1))

    row2 = lambda v: v.reshape(1, -1)
    h_p = x_prompt.reshape(np_, D_MODEL)
    h_s = x_sample.reshape(ns_, D_MODEL)
    nsa_p, nsa_s, mla_p, mla_s, win_p, win_s = [], [], [], [], [], []
    dil_p = [[] for _ in DIL_PATTERNS]
    dil_s = [[] for _ in DIL_PATTERNS]
    y_p = y_s = None

    for i in range(depth):
        l = i // 2
        final = i == depth - 1
        gm = row2(norm_mix[i])
        if i % 2 == 0:
            w_ab = _pack_w_ab(w_in_ab[l])
            w_uq = _pack_w_uq(mla_w_uq[l])
            gq, gkv = row2(mla_q_norm[l]), row2(mla_kv_norm[l])
            pe = jnp.concatenate([nsa_pe_k[l].reshape(CMP_BLOCK, 128), nsa_pe_v[l].reshape(CMP_BLOCK, 128)], axis=1)
            phik, phiv = _block_diag2(nsa_phi_k[l]), _block_diag2(nsa_phi_v[l])
            w_out = w_out_ab[l].astype(BF16)

            t64, tm_, tile = tabs["p"]
            qc, qr, rows, win, qf, mla, misc = _pre_ab(h_p, gm, w_ab, gq, gkv, w_uq, t64, tm_, tile)
            rows3 = rows.reshape(bp, tp, 512)
            kc, vc = _cmp_prompt(rows3, pe, phik, phiv)
            o_a = _nsa_prompt(qc.reshape(bp, tp, -1), qr.reshape(bp, tp, -1), rows3, win.reshape(bp, tp, 256),
                              kc, vc, misc.reshape(bp, tp, LANES), expand)
            kf, vv = _mla_kv(mla, misc, _pack_w_kvup(mla_w_uk[l], mla_w_uv[l]), tile)
            o_b = _mla_prompt(qf.reshape(bp, tp, -1), kf.reshape(bp, tp, -1), vv.reshape(bp, tp, -1))
            mix_p = [o_a.reshape(np_, -1), o_b.reshape(np_, -1)]
            nsa_p.append(rows3)
            mla_p.append(mla.reshape(bp, tp, MLA_ROW))
            win_p.append(win.reshape(bp, tp, 256)[:, max(tp - NSA_WINDOW, 0):])

            t64, tm_, tile = tabs["s"]
            qc, qr, rows, win, qf, mla, misc = _pre_ab(h_s, gm, w_ab, gq, gkv, w_uq, t64, tm_, tile)
            o_cmp, sel_idx = _snsa_a(l, page_table, cache_nsa_t, _stack_group_rows(qc, bs, ts),
                                     pe, phik, phiv, ts)
            sidx = jnp.transpose(sel_idx[:, :, :ts, :N_SEL], (0, 2, 1, 3)).reshape(-1)
            q_tg = jnp.pad(qr.reshape(bs, ts, NSA_KV_HEADS, NSA_GROUP, LANES),
                           ((0, 0), (0, 0), (0, 0), (0, 8 - NSA_GROUP), (0, 0)))
            new_blk = slots_last(_pad_rows(rows.reshape(bs, ts, 512)[:, :, 256:512], LANES))
            o_slc = _snsa_b(l, page_table, sidx, cache_nsa_t, q_tg, new_blk, ts)
            o_win = _snsa_win(l, _stack_group_rows(qr, bs, ts), cache_win_t,
                              slots_last(_pad_rows(win.reshape(bs, ts, 256), LANES)), past_len, ts)
            o_a = _scomb(o_cmp, o_slc, o_win, _pad_rows(misc.reshape(bs, ts, LANES), 8), ts)
            o_a = o_a[:, :ts].reshape(ns_, NSA_HEADS * HEAD_DIM)
            mla_new = slots_last(_pad_rows(mla.reshape(bs, ts, MLA_ROW), LANES))
            o_b = _smla(l, page_table, cache_mla_t, _pad_rows(qf.reshape(bs, ts, -1), SMLA_SLOTS),
                        *_pack_w_absorb(mla_w_uk[l], mla_w_uv[l]), mla_new, ts)
            mix_s = [o_a, o_b[:, :ts].reshape(ns_, -1)]
            nsa_s.append(rows.reshape(bs, ts, 512))
            mla_s.append(mla.reshape(bs, ts, MLA_ROW))
            win_s.append(win.reshape(bs, ts, 256))
        else:
            w_c = w_in_c[l].astype(BF16)
            w_out = w_out_c[l].astype(BF16)

            t64, _, tile = tabs["p"]
            outs = _pre_c(h_p, gm, w_c, t64, tile)
            os_, ls_ = [], []
            for g, (window, dil) in enumerate(DIL_PATTERNS):
                o, lse = _dil_prompt(outs[g], outs[N_DIL + g], bp, tp, dil)
                os_.append(o)
                ls_.append(lse)
                dil_p[g].append(outs[N_DIL + g].reshape(bp, tp, 2 * hw)[:, max(tp - window, 0):])
            mix_p = [_dil_merge(os_, ls_, tile)]

            t64, _, tile = tabs["s"]
            outs = _pre_c(h_s, gm, w_c, t64, tile)
            q_pad = _pad_rows(jnp.concatenate(outs[:N_DIL], axis=1).reshape(bs, ts, -1), 8)
            new_pad = slots_last(_pad_rows(jnp.concatenate(outs[N_DIL:], axis=1).reshape(bs, ts, -1), LANES))
            mix_s = [_sdil(l, q_pad, new_pad, caches_dil_t, ts)[:, :ts].reshape(ns_, hw)]
            for g in range(N_DIL):
                dil_s[g].append(outs[N_DIL + g].reshape(bs, ts, 2 * hw))

        post_w = (w_out, row2(norm_ffn[i]), w_gate_up[i].astype(BF16), w_down[i].astype(BF16),
                  row2(norm_ple[i]), w_ple_gate[i].astype(BF16))
        wpp, gfin = w_ple_proj[i].astype(BF16), row2(norm_final)
        res = _post(h_p, mix_p, *post_w, p_prompt[i].reshape(np_, PLE_DIM), wpp, gfin, final, tabs["p"][2])
        h_p = res[0]
        if final:
            y_p = res[1]
        res = _post(h_s, mix_s, *post_w, p_sample[i].reshape(ns_, PLE_DIM), wpp, gfin, final, tabs["s"][2])
        h_s = res[0]
        if final:
            y_s = res[1]

    def rows_out(parts, b, t, tail):
        return jnp.stack(parts, axis=2).reshape((b, t, len(parts)) + tail)

    def bufs_out(parts, tail):
        x = jnp.stack(parts, axis=0)
        return x.reshape(x.shape[:3] + tail)

    kv_tail = (4, NSA_KV_HEADS, HEAD_DIM)
    win_tail = (2, NSA_KV_HEADS, HEAD_DIM)
    dil_tail = (2, DIL_HEADS, HEAD_DIM)
    out = [y_p.reshape(bp, tp, D_MODEL), y_s.reshape(bs, ts, D_MODEL),
           rows_out(nsa_p, bp, tp, kv_tail), rows_out(nsa_s, bs, ts, kv_tail),
           rows_out(mla_p, bp, tp, (MLA_ROW,)), rows_out(mla_s, bs, ts, (MLA_ROW,)),
           bufs_out(win_p, win_tail), bufs_out(win_s, win_tail)]
    for g in range(N_DIL):
        out += [bufs_out(dil_p[g], dil_tail), bufs_out(dil_s[g], dil_tail)]
    return tuple(out)
```

```python
import functools

import numpy as np
import jax
import jax.numpy as jnp
from jax import lax
from jax.experimental import pallas as pl
from jax.experimental.pallas import tpu as pltpu

F32 = jnp.float32
BF16 = jnp.bfloat16

D_MODEL = 1024
HEAD_DIM = 64
ROT_DIM = HEAD_DIM // 4
ROPE_THETA = 500000.0
RMS_EPS = 1e-6
NSA_HEADS = 8
NSA_KV_HEADS = 2
NSA_GROUP = NSA_HEADS // NSA_KV_HEADS
CMP_BLOCK = 32
SEL_BLOCK = 64
N_SEL = 16
NSA_WINDOW = 512
MLA_HEADS = 8
Q_LORA = 384
KV_LORA = 256
QK_NOPE = 64
QK_ROPE = 32
V_DIM = 64
MLA_ROW = KV_LORA + QK_ROPE
DIL_PATTERNS = ((128, 1), (512, 4), (2048, 16))
N_DIL = len(DIL_PATTERNS)
DIL_HEADS = 8
DIL_BACK = 128
D_FF = ((-(-8 * D_MODEL // 3) + 255) // 256) * 256
PLE_DIM = 256
AB_SPLITS = (NSA_HEADS * HEAD_DIM, 6 * NSA_KV_HEADS * HEAD_DIM, 3 * NSA_HEADS, Q_LORA, KV_LORA, QK_ROPE)

LANES = 128
NEG = -1e30
V7X_VMEM_LIMIT = 56 * 1024 * 1024
ROW_TILE = 256
FF_CHUNK = 256
POST_TILE = 512
NSA_TQ = 512
NSA_TK = 512
MLA_TQ = 1024
MLA_TK = 1024
PAGES_PER_STEP = 32
SMLA_SLOTS = 16


def _dot(a, b):
    return jnp.dot(a.astype(BF16), b.astype(BF16), preferred_element_type=F32)


def _dot_nt(a, b):
    return lax.dot_general(a.astype(BF16), b.astype(BF16), (((1,), (1,)), ((), ())),
                           preferred_element_type=F32)


def _rms(x, g):
    return x * lax.rsqrt(jnp.mean(x * x, axis=-1, keepdims=True) + RMS_EPS) * g


def _tile_lanes(a, reps):
    return a if reps == 1 else jnp.concatenate([a] * reps, axis=1)


def _rope(x, cos, sin_up, sin_dn, half):
    w = x.shape[-1]
    reps = w // LANES
    return (x * _tile_lanes(cos, reps)
            + pltpu.roll(x, half, 1) * _tile_lanes(sin_up, reps)
            + pltpu.roll(x, w - half, 1) * _tile_lanes(sin_dn, reps))


def _rope_tables(pos, rot_dim, period, offset):
    half = rot_dim // 2
    inv = jnp.power(ROPE_THETA, -jnp.arange(half, dtype=F32) / half)
    ang = pos.astype(F32)[:, None] * inv[None]
    cos, sin = jnp.cos(ang), jnp.sin(ang)
    lane = np.arange(LANES) % period - offset
    lo = (lane >= 0) & (lane < half)
    hi = (lane >= half) & (lane < rot_dim)
    j = np.clip(np.where(hi, lane - half, lane), 0, half - 1)
    cos_t = jnp.where(lo | hi, cos[:, j], 1.0)
    sin_up = jnp.where(hi, sin[:, j], 0.0)
    sin_dn = jnp.where(lo, -sin[:, j], 0.0)
    return cos_t, sin_up, sin_dn


def _softmax_cols(pieces):
    m = None
    for s, mask in pieces:
        mm = jnp.max(jnp.where(mask, s, NEG), axis=0, keepdims=True)
        m = mm if m is None else jnp.maximum(m, mm)
    es = [jnp.where(mask, jnp.exp(s - m), 0.0) for s, mask in pieces]
    den = sum(jnp.sum(e, axis=0, keepdims=True) for e in es)
    inv = 1.0 / jnp.maximum(den, 1e-30)
    return [e * inv for e in es], den, m


def _softmax_rows(pieces):
    m = None
    for s, mask in pieces:
        mm = jnp.max(jnp.where(mask, s, NEG), axis=1, keepdims=True)
        m = mm if m is None else jnp.maximum(m, mm)
    es = [jnp.where(mask, jnp.exp(s - m), 0.0) for s, mask in pieces]
    den = sum(jnp.sum(e, axis=1, keepdims=True) for e in es)
    inv = 1.0 / jnp.maximum(den, 1e-30)
    return [e * inv for e in es], den, m


def _flash_update(carry, s_t, mask, v_t):
    m, l, acc = carry
    if mask is not None:
        s_t = jnp.where(mask, s_t, NEG)
    m_new = jnp.maximum(m, jnp.max(s_t, axis=0, keepdims=True))
    p = jnp.exp(s_t - m_new)
    alpha = jnp.exp(m - m_new)
    l_new = alpha * l + jnp.sum(p, axis=0, keepdims=True)
    acc_new = alpha * acc + jnp.dot(v_t, p.astype(BF16), preferred_element_type=F32)
    return m_new, l_new, acc_new


def _params(n_grid):
    return pltpu.CompilerParams(dimension_semantics=("arbitrary",) * n_grid,
                                vmem_limit_bytes=V7X_VMEM_LIMIT)


def _resident(shape):
    nd = len(shape)
    return pl.BlockSpec(shape, lambda *_: (0,) * nd, pipeline_mode=pl.Buffered(1))


def _rows(width, tile):
    return pl.BlockSpec((tile, width), lambda i: (i, 0))


def _pages_per_step(n_pages):
    pps = min(PAGES_PER_STEP, n_pages)
    assert n_pages % pps == 0 and pps % 4 == 0
    return pps


def _pre_ab_kernel(h_ref, gm_ref, w_ref, gq_ref, gkv_ref, wuq_ref,
                   c64_ref, u64_ref, d64_ref, cm_ref, um_ref, dm_ref,
                   qc_ref, qr_ref, rows_ref, win_ref, qf_ref, mla_ref, misc_ref):
    t64 = (c64_ref[...], u64_ref[...], d64_ref[...])
    tm_ = (cm_ref[...], um_ref[...], dm_ref[...])
    hn = _rms(h_ref[...], gm_ref[...])
    z = _dot(hn, w_ref[...])
    nq = NSA_HEADS * LANES
    qc = z[:, 0:nq] * (HEAD_DIM ** -0.5)
    qc_ref[...] = qc
    qr_ref[...] = _rope(qc, *t64, ROT_DIM // 2)
    kv = z[:, nq:nq + 768]
    rows_ref[:, 0:256] = kv[:, 0:256]
    rows_ref[:, 256:384] = _rope(kv[:, 256:384], *t64, ROT_DIM // 2)
    rows_ref[:, 384:512] = kv[:, 384:512]
    win_ref[:, 0:128] = _rope(kv[:, 512:640], *t64, ROT_DIM // 2)
    win_ref[:, 128:256] = kv[:, 640:768]
    o = nq + 768
    qn = _rms(z[:, o:o + Q_LORA], gq_ref[...])
    qf_ref[...] = _rope(_dot(qn, wuq_ref[...]), *tm_, QK_ROPE // 2)
    o += Q_LORA
    ckv = _rms(z[:, o:o + KV_LORA], gkv_ref[...])
    o += KV_LORA
    misc = _rope(z[:, o:o + LANES], *tm_, QK_ROPE // 2)
    mla_ref[:, 0:KV_LORA] = ckv
    mla_ref[:, KV_LORA:MLA_ROW] = misc[:, 0:QK_ROPE]
    misc_ref[...] = misc


def _pre_ab(h, gm, w, gq, gkv, wuq, tabs64, tabsm, tile):
    n = h.shape[0]
    t_tab = tabs64[0].shape[0]
    nt = t_tab // tile
    tab = pl.BlockSpec((tile, LANES), lambda i: (i % nt, 0))
    widths = (NSA_HEADS * LANES, NSA_HEADS * LANES, 512, 256, MLA_HEADS * LANES, MLA_ROW, LANES)
    return pl.pallas_call(
        _pre_ab_kernel,
        grid=(n // tile,),
        in_specs=[_rows(D_MODEL, tile), _resident(gm.shape), _resident(w.shape), _resident(gq.shape),
                  _resident(gkv.shape), _resident(wuq.shape)] + [tab] * 6,
        out_specs=[_rows(wd, tile) for wd in widths],
        out_shape=[jax.ShapeDtypeStruct((n, wd), F32) for wd in widths],
        compiler_params=_params(1),
        name="pre_ab",
    )(h, gm, w, gq, gkv, wuq, *tabs64, *tabsm)


def _mla_kv_kernel(mla_ref, misc_ref, w_ref, k_ref, v_ref):
    x = jnp.concatenate([mla_ref[:, 0:KV_LORA], misc_ref[...]], axis=1)
    z = _dot(x, w_ref[...])
    k_ref[...] = z[:, 0:MLA_HEADS * LANES]
    v_ref[...] = z[:, MLA_HEADS * LANES:]


def _mla_kv(mla, misc, w, tile):
    n = mla.shape[0]
    return pl.pallas_call(
        _mla_kv_kernel,
        grid=(n // tile,),
        in_specs=[_rows(MLA_ROW, tile), _rows(LANES, tile), _resident(w.shape)],
        out_specs=[_rows(MLA_HEADS * LANES, tile), _rows(MLA_HEADS * V_DIM, tile)],
        out_shape=[jax.ShapeDtypeStruct((n, MLA_HEADS * LANES), F32),
                   jax.ShapeDtypeStruct((n, MLA_HEADS * V_DIM), F32)],
        compiler_params=_params(1),
        name="mla_kv",
    )(mla, misc, w)


def _pre_c_kernel(h_ref, gm_ref, w_ref, c64_ref, u64_ref, d64_ref, *out_refs):
    t64 = (c64_ref[...], u64_ref[...], d64_ref[...])
    hn = _rms(h_ref[...], gm_ref[...])
    z = _dot(hn, w_ref[...])
    hw = DIL_HEADS * HEAD_DIM
    for g in range(N_DIL):
        o = g * 3 * hw
        out_refs[g][...] = _rope(z[:, o:o + hw] * (HEAD_DIM ** -0.5), *t64, ROT_DIM // 2)
        out_refs[N_DIL + g][:, 0:hw] = _rope(z[:, o + hw:o + 2 * hw], *t64, ROT_DIM // 2)
        out_refs[N_DIL + g][:, hw:2 * hw] = z[:, o + 2 * hw:o + 3 * hw]


def _pre_c(h, gm, w, tabs64, tile):
    n = h.shape[0]
    nt = tabs64[0].shape[0] // tile
    tab = pl.BlockSpec((tile, LANES), lambda i: (i % nt, 0))
    hw = DIL_HEADS * HEAD_DIM
    widths = (hw,) * N_DIL + (2 * hw,) * N_DIL
    return pl.pallas_call(
        _pre_c_kernel,
        grid=(n // tile,),
        in_specs=[_rows(D_MODEL, tile), _resident(gm.shape), _resident(w.shape)] + [tab] * 3,
        out_specs=[_rows(wd, tile) for wd in widths],
        out_shape=[jax.ShapeDtypeStruct((n, wd), F32) for wd in widths],
        compiler_params=_params(1),
        name="pre_c",
    )(h, gm, w, *tabs64)


def _post_kernel(final, n_mix, h_ref, *refs):
    a_refs = refs[:n_mix]
    (wout_ref, gffn_ref, wgu_ref, wd_ref, gple_ref, wpg_ref, ple_ref, wpp_ref, gfin_ref, o_ref) = refs[n_mix:n_mix + 10]
    y_ref = refs[n_mix + 10:]
    h1 = h_ref[...]
    off = 0
    for a_ref in a_refs:
        h1 = h1 + _dot(a_ref[...], wout_ref[off:off + a_ref.shape[1], :])
        off += a_ref.shape[1]
    xn = _rms(h1, gffn_ref[...]).astype(BF16)
    acc = jnp.zeros(h1.shape, F32)
    for c in range(D_FF // FF_CHUNK):
        lo = c * FF_CHUNK
        g = jnp.dot(xn, wgu_ref[:, lo:lo + FF_CHUNK], preferred_element_type=F32)
        u = jnp.dot(xn, wgu_ref[:, D_FF + lo:D_FF + lo + FF_CHUNK], preferred_element_type=F32)
        a = g * jax.nn.sigmoid(g) * u
        acc = acc + jnp.dot(a.astype(BF16), wd_ref[lo:lo + FF_CHUNK, :], preferred_element_type=F32)
    h2 = h1 + acc
    gate = jax.nn.sigmoid(_dot(_rms(h2, gple_ref[...]), wpg_ref[...]))
    h3 = h2 + gate * _dot(ple_ref[...], wpp_ref[...])
    o_ref[...] = h3
    if final:
        y_ref[0][...] = _rms(h3, gfin_ref[...])


def _post(h, mix, wout, gffn, wgu, wd, gple, wpg, ple, wpp, gfin, final, tile):
    n = h.shape[0]
    n_out = 2 if final else 1
    return pl.pallas_call(
        functools.partial(_post_kernel, final, len(mix)),
        grid=(n // tile,),
        in_specs=[_rows(D_MODEL, tile)] + [_rows(a.shape[1], tile) for a in mix] + [
                  _resident(wout.shape), _resident(gffn.shape),
                  _resident(wgu.shape), _resident(wd.shape), _resident(gple.shape), _resident(wpg.shape),
                  _rows(PLE_DIM, tile), _resident(wpp.shape), _resident(gfin.shape)],
        out_specs=[_rows(D_MODEL, tile)] * n_out,
        out_shape=[jax.ShapeDtypeStruct((n, D_MODEL), F32)] * n_out,
        compiler_params=_params(1),
        name="post",
    )(h, *mix, wout, gffn, wgu, wd, gple, wpg, ple, wpp, gfin)


def _cmp_kernel(rows_ref, pe_ref, phik_ref, phiv_ref, kc_ref, vc_ref):
    t = rows_ref.shape[1]
    nsb = t // SEL_BLOCK
    x3 = rows_ref[0, :, 0:256].reshape(nsb, SEL_BLOCK, 256)
    pe = pe_ref[...][None]
    even = jnp.mean(x3[:, 0:CMP_BLOCK, :] + pe, axis=1)
    odd = jnp.mean(x3[:, CMP_BLOCK:SEL_BLOCK, :] + pe, axis=1)
    km = jnp.concatenate([even, odd], axis=0)
    kc_ref[0] = _dot(km[:, 0:128], phik_ref[...])
    vc_ref[0] = _dot(km[:, 128:256], phiv_ref[...])


def _cmp_prompt(rows3, pe, phik, phiv):
    b, t, _ = rows3.shape
    ncb = t // CMP_BLOCK
    return pl.pallas_call(
        _cmp_kernel,
        grid=(b,),
        in_specs=[pl.BlockSpec((1, t, 512), lambda i: (i, 0, 0)), _resident(pe.shape),
                  _resident(phik.shape), _resident(phiv.shape)],
        out_specs=[pl.BlockSpec((1, ncb, LANES), lambda i: (i, 0, 0))] * 2,
        out_shape=[jax.ShapeDtypeStruct((b, ncb, LANES), F32)] * 2,
        compiler_params=_params(1),
        name="nsa_cmp",
    )(rows3, pe, phik, phiv)


def _nsa_prompt_kernel(qc_ref, qr_ref, rows_ref, win_ref, kc_ref, vc_ref, misc_ref, exp_ref, o_ref):
    qi = pl.program_id(1)
    tq = qc_ref.shape[1]
    t_len = rows_ref.shape[1]
    nsb = t_len // SEL_BLOCK
    ncb = 2 * nsb
    q0 = qi * tq
    gates_t = jax.nn.sigmoid(misc_ref[0].T)
    kc = kc_ref[0].astype(BF16)
    vc_t = vc_ref[0].T.astype(BF16)
    qpos1 = q0 + lax.broadcasted_iota(jnp.int32, (1, tq), 1)
    qpos4 = _tile_lanes(qpos1, NSA_GROUP)
    out_rows = []
    for g in range(NSA_KV_HEADS):
        heads = range(g * NSA_GROUP, (g + 1) * NSA_GROUP)
        qc_g = jnp.concatenate([qc_ref[0, :, h * LANES:(h + 1) * LANES] for h in heads], axis=0).astype(BF16)
        qr_g = jnp.concatenate([qr_ref[0, :, h * LANES:(h + 1) * LANES] for h in heads], axis=0).astype(BF16)
        s_c = _dot_nt(kc, qc_g)
        n_idx = lax.broadcasted_iota(jnp.int32, (ncb, NSA_GROUP * tq), 0)
        n_orig = jnp.where(n_idx < nsb, 2 * n_idx, 2 * (n_idx - nsb) + 1)
        cmask = (n_orig + 1) * CMP_BLOCK - 1 <= qpos4
        (p_c,), _, _ = _softmax_cols([(s_c, cmask)])
        o_c = jnp.dot(vc_t, p_c.astype(BF16), preferred_element_type=F32)[g * HEAD_DIM:(g + 1) * HEAD_DIM]
        imp = sum(p_c[:, r * tq:(r + 1) * tq] for r in range(NSA_GROUP))
        imp = imp[0:nsb] + imp[nsb:ncb]
        blk = lax.broadcasted_iota(jnp.int32, (nsb, tq), 0)
        forced = (blk == 0) | (blk == (qpos1 >> 6))
        val = jnp.where(forced, jnp.inf, jnp.where(blk * SEL_BLOCK <= qpos1, imp, -jnp.inf))
        rank = jnp.zeros((nsb, tq), F32)
        for i in range(nsb):
            vi = val[i:i + 1, :]
            rank = rank + jnp.where(vi > val, 1.0, 0.0) + jnp.where(vi == val, jnp.where(blk > i, 1.0, 0.0), 0.0)
        sel = jnp.where(rank < min(N_SEL, nsb), jnp.where(val > -jnp.inf, 1.0, 0.0), 0.0).astype(BF16)

        init = (jnp.full((1, NSA_GROUP * tq), NEG, F32), jnp.zeros((1, NSA_GROUP * tq), F32),
                jnp.zeros((HEAD_DIM, NSA_GROUP * tq), F32))

        def slc_body(kt, carry):
            k0 = pl.multiple_of(kt * NSA_TK, NSA_TK)
            k = rows_ref[0, pl.ds(k0, NSA_TK), 256:384]
            v = rows_ref[0, pl.ds(k0, NSA_TK), 384:512]
            s_t = _dot_nt(k, qr_g)
            chosen = jnp.dot(exp_ref[pl.ds(k0, NSA_TK), :], sel, preferred_element_type=F32)
            kpos = k0 + lax.broadcasted_iota(jnp.int32, (NSA_TK, tq), 0)
            mask = _tile_lanes((chosen > 0.5) & (kpos <= qpos1), NSA_GROUP)
            v_t = v.T[g * HEAD_DIM:(g + 1) * HEAD_DIM].astype(BF16)
            return _flash_update(carry, s_t, mask, v_t)

        n_kt = (q0 + tq - 1) // NSA_TK + 1
        _, l_s, acc_s = lax.fori_loop(0, n_kt, slc_body, init)
        o_s = acc_s / l_s

        carry = init
        n_back_tiles = -(-(NSA_WINDOW - 1) // tq)
        for j in range(n_back_tiles + 1):
            kt = qi - j
            k0 = pl.multiple_of(jnp.maximum(kt, 0) * tq, tq)
            k = win_ref[0, pl.ds(k0, tq), 0:128]
            v = win_ref[0, pl.ds(k0, tq), 128:256]
            s_t = _dot_nt(k, qr_g)
            kpos = kt * tq + lax.broadcasted_iota(jnp.int32, (tq, tq), 0)
            mask = _tile_lanes((kpos <= qpos1) & (kpos >= qpos1 - (NSA_WINDOW - 1)) & (kpos >= 0), NSA_GROUP)
            v_t = v.T[g * HEAD_DIM:(g + 1) * HEAD_DIM].astype(BF16)
            carry = _flash_update(carry, s_t, mask, v_t)
        o_w = carry[2] / carry[1]

        for r, h in enumerate(heads):
            sl = slice(r * tq, (r + 1) * tq)
            row = 32 + h
            out_rows.append(gates_t[row:row + 1] * o_c[:, sl]
                            + gates_t[row + 8:row + 9] * o_s[:, sl]
                            + gates_t[row + 16:row + 17] * o_w[:, sl])
    o_ref[0] = jnp.concatenate(out_rows, axis=0).T


def _nsa_prompt(qc3, qr3, rows3, win3, kc, vc, misc3, expand):
    b, t, _ = rows3.shape
    tq = min(NSA_TQ, t)
    ncb = kc.shape[1]
    qspec = pl.BlockSpec((1, tq, NSA_HEADS * LANES), lambda i, j: (i, j, 0))
    return pl.pallas_call(
        _nsa_prompt_kernel,
        grid=(b, t // tq),
        in_specs=[qspec, qspec,
                  pl.BlockSpec((1, t, 512), lambda i, j: (i, 0, 0)),
                  pl.BlockSpec((1, t, 256), lambda i, j: (i, 0, 0)),
                  pl.BlockSpec((1, ncb, LANES), lambda i, j: (i, 0, 0)),
                  pl.BlockSpec((1, ncb, LANES), lambda i, j: (i, 0, 0)),
                  pl.BlockSpec((1, tq, LANES), lambda i, j: (i, j, 0)),
                  _resident(expand.shape)],
        out_specs=pl.BlockSpec((1, tq, NSA_HEADS * HEAD_DIM), lambda i, j: (i, j, 0)),
        out_shape=jax.ShapeDtypeStruct((b, t, NSA_HEADS * HEAD_DIM), F32),
        compiler_params=_params(2),
        name="nsa_prompt",
    )(qc3, qr3, rows3, win3, kc, vc, misc3, expand)


def _mla_prompt_kernel(q_ref, k_ref, v_ref, o_ref):
    qi = pl.program_id(2)
    tq = q_ref.shape[1]
    tk = min(MLA_TK, k_ref.shape[1])
    q0 = qi * tq
    qpos = q0 + lax.broadcasted_iota(jnp.int32, (1, tq), 1)
    scale = (QK_NOPE + QK_ROPE) ** -0.5
    outs = []
    n_kt = (q0 + tq - 1) // tk + 1
    for hh in range(2):
        q = (q_ref[0, :, hh * LANES:(hh + 1) * LANES] * scale).astype(BF16)

        def tile(kt, carry, causal):
            k0 = pl.multiple_of(kt * tk, tk)
            k = k_ref[0, pl.ds(k0, tk), hh * LANES:(hh + 1) * LANES]
            s_t = _dot_nt(k, q)
            mask = (k0 + lax.broadcasted_iota(jnp.int32, (tk, tq), 0) <= qpos) if causal else None
            v_t = v_ref[0, pl.ds(k0, tk), :].T[hh * V_DIM:(hh + 1) * V_DIM].astype(BF16)
            return _flash_update(carry, s_t, mask, v_t)

        init = (jnp.full((1, tq), NEG, F32), jnp.zeros((1, tq), F32), jnp.zeros((V_DIM, tq), F32))
        carry = lax.fori_loop(0, n_kt - 1, lambda kt, c: tile(kt, c, False), init)
        _, l, acc = tile(n_kt - 1, carry, True)
        outs.append(acc / l)
    o_ref[0] = jnp.concatenate(outs, axis=0).T


def _mla_prompt(qf3, kf3, v3):
    b, t, _ = qf3.shape
    tq = min(MLA_TQ, t)
    return pl.pallas_call(
        _mla_prompt_kernel,
        grid=(b, MLA_HEADS // 2, t // tq),
        in_specs=[pl.BlockSpec((1, tq, 2 * LANES), lambda i, p, j: (i, j, p)),
                  pl.BlockSpec((1, t, 2 * LANES), lambda i, p, j: (i, 0, p)),
                  pl.BlockSpec((1, t, 2 * V_DIM), lambda i, p, j: (i, 0, p))],
        out_specs=pl.BlockSpec((1, tq, 2 * V_DIM), lambda i, p, j: (i, j, p)),
        out_shape=jax.ShapeDtypeStruct((b, t, MLA_HEADS * V_DIM), F32),
        compiler_params=_params(3),
        name="mla_prompt",
    )(qf3, kf3, v3)


def _dil_prompt_kernel(sub, has_prev, q_ref, kc_ref, vc_ref, kp_ref, vp_ref, o_ref, lse_ref):
    i = pl.program_id(2)
    tq = q_ref.shape[1]
    lane = lax.broadcasted_iota(jnp.int32, (1, LANES), 1)
    half_masks = [jnp.where(lane < HEAD_DIM, 1.0, 0.0), jnp.where(lane >= HEAD_DIM, 1.0, 0.0)]
    n_keys = 2 * sub if has_prev else sub
    c_idx = lax.broadcasted_iota(jnp.int32, (n_keys, sub), 0)
    q_idx = lax.broadcasted_iota(jnp.int32, (n_keys, sub), 1)
    kpos = c_idx - sub if has_prev else c_idx
    band = (kpos <= q_idx) & (kpos >= q_idx - DIL_BACK)
    for s in range(tq // sub):
        rs = slice(s * sub, (s + 1) * sub)
        q = q_ref[0, rs, :]
        k = kc_ref[0, rs, :]
        v = vc_ref[0, rs, :]
        mask = band
        if has_prev:
            if s == 0:
                k = jnp.concatenate([kp_ref[0], k], axis=0)
                v = jnp.concatenate([vp_ref[0], v], axis=0)
                mask = band & (c_idx >= jnp.where(i > 0, 0, sub))
            else:
                ps = slice((s - 1) * sub, s * sub)
                k = jnp.concatenate([kc_ref[0, ps, :], k], axis=0)
                v = jnp.concatenate([vc_ref[0, ps, :], v], axis=0)
        o_rows, lse_rows = [], []
        mask2 = _tile_lanes(mask, 2)
        for j in range(DIL_HEADS // 2):
            cs = slice(j * LANES, (j + 1) * LANES)
            v_t = v[:, cs].T.astype(BF16)
            q2 = jnp.concatenate([q[:, cs] * half_masks[0], q[:, cs] * half_masks[1]], axis=0)
            s_t = _dot_nt(k[:, cs], q2)
            (p,), den, m = _softmax_cols([(s_t, mask2)])
            o2 = jnp.dot(v_t, p.astype(BF16), preferred_element_type=F32)
            lse2 = jnp.log(den) + m
            for hh in range(2):
                o_rows.append(o2[hh * HEAD_DIM:(hh + 1) * HEAD_DIM, hh * sub:(hh + 1) * sub])
                lse_rows.append(jnp.broadcast_to(lse2[:, hh * sub:(hh + 1) * sub], (HEAD_DIM, sub)))
        o_ref[0, rs, :] = jnp.concatenate(o_rows, axis=0).T
        lse_ref[0, rs, :] = jnp.concatenate(lse_rows, axis=0).T


def _dil_prompt(q, kv, b, t, dil):
    hw = DIL_HEADS * HEAD_DIM
    l = t // dil
    sub = min(LANES, l)
    has_prev = l > sub
    tq = min(l, 4 * sub)
    per = tq // sub
    q3 = q.reshape(b, l, dil * hw)
    kv3 = kv.reshape(b, l, dil * 2 * hw)
    cur = lambda off: pl.BlockSpec((1, tq, hw), lambda bi, r, i: (bi, i, 2 * r + off))
    prev = lambda off: pl.BlockSpec((1, sub, hw), lambda bi, r, i: (bi, jnp.maximum(i * per - 1, 0), 2 * r + off))
    qo = pl.BlockSpec((1, tq, hw), lambda bi, r, i: (bi, i, r))
    o, lse = pl.pallas_call(
        functools.partial(_dil_prompt_kernel, sub, has_prev),
        grid=(b, dil, l // tq),
        in_specs=[qo, cur(0), cur(1), prev(0), prev(1)],
        out_specs=[qo, qo],
        out_shape=[jax.ShapeDtypeStruct((b, l, dil * hw), F32)] * 2,
        compiler_params=_params(3),
        name="dil_prompt",
    )(q3, kv3, kv3, kv3, kv3)
    return o.reshape(b * t, hw), lse.reshape(b * t, hw)


def _dil_merge_kernel(o0, o1, o2, l0, l1, l2, out_ref):
    ls = [l0[...], l1[...], l2[...]]
    mx = jnp.maximum(jnp.maximum(ls[0], ls[1]), ls[2])
    ws = [jnp.exp(x - mx) for x in ls]
    num = ws[0] * o0[...] + ws[1] * o1[...] + ws[2] * o2[...]
    out_ref[...] = num / (ws[0] + ws[1] + ws[2])


def _dil_merge(outs, lses, tile):
    n, hw = outs[0].shape
    return pl.pallas_call(
        _dil_merge_kernel,
        grid=(n // tile,),
        in_specs=[_rows(hw, tile)] * 6,
        out_specs=_rows(hw, tile),
        out_shape=jax.ShapeDtypeStruct((n, hw), F32),
        compiler_params=_params(1),
        name="dil_merge",
    )(*outs, *lses)


def _snsa_a_kernel(pps, past_len, t_new, pt_ref, *refs):
    pages = refs[:pps]
    qc_ref, pe_ref, phik_ref, phiv_ref, ocmp_ref, idx_ref, cme_ref, cmo_ref = refs[pps:]
    step = pl.program_id(1)
    n_steps = pl.num_programs(1)
    pe = pe_ref[...][None]
    for quad in range(pps // 4):
        x = jnp.concatenate([pages[4 * quad + j][0, 0].T for j in range(4)], axis=0)
        x3 = x.reshape(8, SEL_BLOCK, 256)
        r0 = pl.multiple_of(step * (2 * pps) + quad * 8, 8)
        cme_ref[pl.ds(r0, 8), :] = jnp.mean(x3[:, 0:CMP_BLOCK, :] + pe, axis=1)
        cmo_ref[pl.ds(r0, 8), :] = jnp.mean(x3[:, CMP_BLOCK:SEL_BLOCK, :] + pe, axis=1)

    @pl.when(step == n_steps - 1)
    def _():
        nsb_past = cme_ref.shape[0]
        ncb = 2 * nsb_past
        km = jnp.concatenate([cme_ref[...], cmo_ref[...]], axis=0)
        kc = _dot(km[:, 0:128], phik_ref[...]).astype(BF16)
        vc = _dot(km[:, 128:256], phiv_ref[...]).astype(BF16)
        rows = NSA_GROUP * 8
        t_row = lax.broadcasted_iota(jnp.int32, (rows, 1), 0) & 7
        pos = past_len + t_row
        n_idx = lax.broadcasted_iota(jnp.int32, (rows, ncb), 1)
        n_orig = jnp.where(n_idx < nsb_past, 2 * n_idx, 2 * (n_idx - nsb_past) + 1)
        cmask = (n_orig + 1) * CMP_BLOCK - 1 <= pos
        nsb = -(-(past_len + t_new) // SEL_BLOCK)
        wide = -(-nsb // LANES) * LANES
        blk = lax.broadcasted_iota(jnp.int32, (8, wide), 1)
        blk_f = blk.astype(F32)
        pos8 = past_len + lax.broadcasted_iota(jnp.int32, (8, 1), 0)
        lane = lax.broadcasted_iota(jnp.int32, (8, LANES), 1)
        for g in range(NSA_KV_HEADS):
            s = _dot_nt(qc_ref[0, g], kc)
            (p,), _, _ = _softmax_rows([(s, cmask)])
            ocmp_ref[0, g] = jnp.dot(p.astype(BF16), vc, preferred_element_type=F32)
            imp = sum(p[r * 8:(r + 1) * 8, 0:nsb_past] + p[r * 8:(r + 1) * 8, nsb_past:ncb]
                      for r in range(NSA_GROUP))
            imp = jnp.concatenate([imp, jnp.zeros((8, wide - nsb_past), F32)], axis=1)
            forced = (blk == 0) | (blk == (pos8 >> 6))
            val = jnp.where(forced, jnp.inf, jnp.where(blk * SEL_BLOCK <= pos8, imp, -jnp.inf))
            picks = jnp.full((8, LANES), -1.0, F32)
            for k in range(min(N_SEL, nsb)):
                mx = jnp.max(val, axis=1, keepdims=True)
                first = jnp.min(jnp.where(val == mx, blk_f, 1e9), axis=1, keepdims=True)
                picks = jnp.where(lane == k, jnp.where(mx > -jnp.inf, first, -1.0), picks)
                val = jnp.where(blk_f == first, -jnp.inf, val)
            idx_ref[0, g] = picks.astype(jnp.int32)


def _snsa_a(layer, pt, cache_v, qc_st, pe, phik, phiv, t_new):
    b, n_pages = pt.shape
    past_len = n_pages * LANES
    nsb_past = past_len // SEL_BLOCK
    pps = _pages_per_step(n_pages)
    page = lambda j: pl.BlockSpec(
        (1, 1, 256, LANES), lambda bi, s, pt_ref: (pt_ref[bi, s * pps + j], layer, 0, 0))
    const = lambda shape: pl.BlockSpec(shape, lambda bi, s, pt_ref: (0,) * len(shape))
    grid_spec = pltpu.PrefetchScalarGridSpec(
        num_scalar_prefetch=1,
        grid=(b, n_pages // pps),
        in_specs=[page(j) for j in range(pps)] + [
            pl.BlockSpec((1, NSA_KV_HEADS, NSA_GROUP * 8, LANES), lambda bi, s, pt_ref: (bi, 0, 0, 0)),
            const(pe.shape), const(phik.shape), const(phiv.shape)],
        out_specs=[pl.BlockSpec((1, NSA_KV_HEADS, NSA_GROUP * 8, LANES), lambda bi, s, pt_ref: (bi, 0, 0, 0)),
                   pl.BlockSpec((1, NSA_KV_HEADS, 8, LANES), lambda bi, s, pt_ref: (bi, 0, 0, 0))],
        scratch_shapes=[pltpu.VMEM((nsb_past, 256), F32), pltpu.VMEM((nsb_past, 256), F32)],
    )
    return pl.pallas_call(
        functools.partial(_snsa_a_kernel, pps, past_len, t_new),
        grid_spec=grid_spec,
        out_shape=[jax.ShapeDtypeStruct((b, NSA_KV_HEADS, NSA_GROUP * 8, LANES), F32),
                   jax.ShapeDtypeStruct((b, NSA_KV_HEADS, 8, LANES), jnp.int32)],
        compiler_params=_params(2),
        name="snsa_cmp_topk",
    )(pt, *([cache_v] * pps), qc_st, pe, phik, phiv)


def _snsa_b_kernel(past_len, t_new, n_t, pt_ref, sidx_ref, *refs):
    n_sel = N_SEL
    k_blocks, v_blocks = refs[:n_sel], refs[n_sel:2 * n_sel]
    q_ref, knew_ref, vnew_ref, o_ref = refs[2 * n_sel:]
    del pt_ref
    bi, ti, gi = pl.program_id(0), pl.program_id(1), pl.program_id(2)
    base = ((bi * n_t + ti) * NSA_KV_HEADS + gi) * n_sel
    new_blk = past_len // SEL_BLOCK
    pos = past_len + ti
    q = q_ref[0, 0, 0]
    q = (q[:, 0:HEAD_DIM] + q[:, HEAD_DIM:LANES]).astype(BF16)
    k_new = knew_ref[0]
    v_new = vnew_ref[0]
    lane = lax.broadcasted_iota(jnp.int32, (8, LANES), 1)
    pieces, vals = [], []
    for j in range(n_sel):
        idx = sidx_ref[base + j]
        is_new = idx == new_blk
        k_t = jnp.where(is_new, k_new, k_blocks[j][0, 0]).astype(BF16)
        vals.append(jnp.where(is_new, v_new, v_blocks[j][0, 0]).astype(BF16))
        s = jnp.dot(q, k_t, preferred_element_type=F32)
        idx_v = lane * 0 + idx
        kpos = (idx_v >> 1) * LANES + lane
        in_block = (lane >> 6) == (idx_v & 1)
        pieces.append((s, (idx_v >= 0) & in_block & (kpos <= pos) & (kpos < past_len + t_new)))
    ps, _, _ = _softmax_rows(pieces)
    o = sum(_dot_nt(p, v) for p, v in zip(ps, vals))
    o_ref[0, 0, 0] = jnp.concatenate([o, o], axis=1)


def _snsa_b(layer, pt, sidx, cache_t, q_tg, new_blk_t, t_new):
    b, n_pages = pt.shape
    past_len = n_pages * LANES
    n_cache_blk = past_len // SEL_BLOCK
    k_first, v_first = 2 * NSA_KV_HEADS, 3 * NSA_KV_HEADS

    def blk_spec(j, first):
        def imap(bi, ti, gi, pt_ref, sidx_ref):
            idx = sidx_ref[((bi * t_new + ti) * NSA_KV_HEADS + gi) * N_SEL + j]
            idx = jnp.clip(idx, 0, n_cache_blk - 1)
            return (pt_ref[bi, idx // 2], layer, first + gi, 0)
        return pl.BlockSpec((1, 1, HEAD_DIM, LANES), imap)

    new_spec = lambda first: pl.BlockSpec((1, HEAD_DIM, LANES), lambda bi, ti, gi, *_: (bi, first + gi, 0))
    grid_spec = pltpu.PrefetchScalarGridSpec(
        num_scalar_prefetch=2,
        grid=(b, t_new, NSA_KV_HEADS),
        in_specs=[blk_spec(j, k_first) for j in range(N_SEL)] + [blk_spec(j, v_first) for j in range(N_SEL)] + [
            pl.BlockSpec((1, 1, 1, 8, LANES), lambda bi, ti, gi, *_: (bi, ti, gi, 0, 0)),
            new_spec(0), new_spec(NSA_KV_HEADS)],
        out_specs=pl.BlockSpec((1, 1, 1, 8, LANES), lambda bi, ti, gi, *_: (bi, ti, gi, 0, 0)),
    )
    return pl.pallas_call(
        functools.partial(_snsa_b_kernel, past_len, t_new, t_new),
        grid_spec=grid_spec,
        out_shape=jax.ShapeDtypeStruct((b, t_new, NSA_KV_HEADS, 8, LANES), F32),
        compiler_params=_params(3),
        name="snsa_slc",
    )(pt, sidx, *([cache_t] * (2 * N_SEL)), q_tg, new_blk_t, new_blk_t)


def _snsa_win_kernel(past_len, t_new, q_ref, c_ref, new_ref, o_ref):
    lw = c_ref.shape[3]
    rows = NSA_GROUP * 8
    t_row = lax.broadcasted_iota(jnp.int32, (rows, 1), 0) & 7
    pos = past_len + t_row
    kpos_c = (past_len - lw) + lax.broadcasted_iota(jnp.int32, (rows, lw), 1)
    c_new = lax.broadcasted_iota(jnp.int32, (rows, LANES), 1)
    kpos_n = past_len + c_new
    mask_c = (kpos_c <= pos) & (kpos_c >= pos - (NSA_WINDOW - 1))
    mask_n = (c_new < t_new) & (kpos_n <= pos) & (kpos_n >= pos - (NSA_WINDOW - 1))
    kc, vc = c_ref[0, 0, 0:128, :], c_ref[0, 0, 128:256, :]
    kn, vn = new_ref[0, 0:128, :], new_ref[0, 128:256, :]
    for g in range(NSA_KV_HEADS):
        q = q_ref[0, g]
        (p1, p2), _, _ = _softmax_rows([(_dot(q, kc), mask_c), (_dot(q, kn), mask_n)])
        o_ref[0, g] = _dot_nt(p1, vc) + _dot_nt(p2, vn)


def _snsa_win(layer, qr_st, cache_win, win_new, past_len, t_new):
    b = qr_st.shape[0]
    lw = cache_win.shape[3]
    return pl.pallas_call(
        functools.partial(_snsa_win_kernel, past_len, t_new),
        grid=(b,),
        in_specs=[pl.BlockSpec((1, NSA_KV_HEADS, NSA_GROUP * 8, LANES), lambda i: (i, 0, 0, 0)),
                  pl.BlockSpec((1, 1, 256, lw), lambda i: (layer, i, 0, 0)),
                  pl.BlockSpec((1, 256, LANES), lambda i: (i, 0, 0))],
        out_specs=pl.BlockSpec((1, NSA_KV_HEADS, NSA_GROUP * 8, LANES), lambda i: (i, 0, 0, 0)),
        out_shape=jax.ShapeDtypeStruct((b, NSA_KV_HEADS, NSA_GROUP * 8, LANES), F32),
        compiler_params=_params(1),
        name="snsa_win",
    )(qr_st, cache_win, win_new)


def _scomb_kernel(t_new, oc_ref, os_ref, ow_ref, misc_ref, o_ref):
    gates_t = jax.nn.sigmoid(misc_ref[0].T)
    pad = jnp.zeros((8 - t_new, LANES), F32)
    rows = []
    for g in range(NSA_KV_HEADS):
        gs = slice(g * HEAD_DIM, (g + 1) * HEAD_DIM)
        for r in range(NSA_GROUP):
            h = g * NSA_GROUP + r
            oc = oc_ref[0, g, r * 8:(r + 1) * 8, :].T[gs]
            ow = ow_ref[0, g, r * 8:(r + 1) * 8, :].T[gs]
            os_ = jnp.concatenate([os_ref[0, t, g, r:r + 1, :] for t in range(t_new)] + [pad], axis=0).T[gs]
            row = 32 + h
            rows.append(gates_t[row:row + 1] * oc + gates_t[row + 8:row + 9] * os_ + gates_t[row + 16:row + 17] * ow)
    o_ref[0] = jnp.concatenate(rows, axis=0).T


def _scomb(oc, os_, ow, misc_pad, t_new):
    b = oc.shape[0]
    grp = pl.BlockSpec((1, NSA_KV_HEADS, NSA_GROUP * 8, LANES), lambda i: (i, 0, 0, 0))
    return pl.pallas_call(
        functools.partial(_scomb_kernel, t_new),
        grid=(b,),
        in_specs=[grp, pl.BlockSpec((1, t_new, NSA_KV_HEADS, 8, LANES), lambda i: (i, 0, 0, 0, 0)), grp,
                  pl.BlockSpec((1, 8, LANES), lambda i: (i, 0, 0))],
        out_specs=pl.BlockSpec((1, 8, NSA_HEADS * HEAD_DIM), lambda i: (i, 0, 0)),
        out_shape=jax.ShapeDtypeStruct((b, 8, NSA_HEADS * HEAD_DIM), F32),
        compiler_params=_params(1),
        name="snsa_combine",
    )(oc, os_, ow, misc_pad)


def _col_to_row(col):
    n = col.shape[0]
    return jnp.broadcast_to(col, (n, LANES)).T[0:1, 0:n]


def _smla_kernel(pps, t_new, pt_ref, *refs):
    pages = refs[:pps]
    (qf_ref, wuka_ref, wukb_ref, wuv_ref, new_ref, o_ref,
     qa_ref, qb_ref, m_ref, l_ref, acc_ref) = refs[pps:]
    del pt_ref
    step = pl.program_id(1)
    n_steps = pl.num_programs(1)
    scale = (QK_NOPE + QK_ROPE) ** -0.5
    rows = MLA_HEADS * SMLA_SLOTS
    tail = MLA_ROW - LANES

    @pl.when(step == 0)
    def _():
        for h in range(MLA_HEADS):
            hs = slice(h * SMLA_SLOTS, (h + 1) * SMLA_SLOTS)
            qh = qf_ref[0, :, h * LANES:(h + 1) * LANES].astype(BF16)
            qa_ref[hs, :] = jnp.dot(qh, wuka_ref[h], preferred_element_type=F32)
            qb_ref[hs, :] = jnp.dot(qh, wukb_ref[h], preferred_element_type=F32)
        m_ref[...] = jnp.full(m_ref.shape, NEG, F32)
        l_ref[...] = jnp.zeros(l_ref.shape, F32)
        acc_ref[...] = jnp.zeros(acc_ref.shape, F32)

    qa = qa_ref[...].astype(BF16)
    qb = qb_ref[...].astype(BF16)

    def absorb(x_t, mask):
        s = (jnp.dot(qa, x_t[0:KV_LORA], preferred_element_type=F32)
             + jnp.dot(qb, x_t[tail:MLA_ROW], preferred_element_type=F32)) * scale
        m_old = m_ref[...]
        s_m = s if mask is None else jnp.where(mask, s, NEG)
        m_new = jnp.maximum(m_old, jnp.max(s_m, axis=1, keepdims=True))
        p = jnp.exp(s - m_new)
        if mask is not None:
            p = jnp.where(mask, p, 0.0)
        alpha = jnp.exp(m_old - m_new)
        l_ref[...] = alpha * l_ref[...] + jnp.sum(p, axis=1, keepdims=True)
        acc_ref[...] = (acc_ref[...] * _col_to_row(alpha)
                        + jnp.dot(x_t, p.T.astype(BF16), preferred_element_type=F32))
        m_ref[...] = m_new

    absorb(jnp.concatenate([pages[j][0, 0].astype(BF16) for j in range(pps)], axis=1), None)

    @pl.when(step == n_steps - 1)
    def _():
        c = lax.broadcasted_iota(jnp.int32, (rows, LANES), 1)
        t_row = lax.broadcasted_iota(jnp.int32, (rows, LANES), 0) & (SMLA_SLOTS - 1)
        absorb(new_ref[0].astype(BF16), (c < t_new) & (c <= t_row))
        o_lat = (acc_ref[0:KV_LORA, :] * _col_to_row(1.0 / l_ref[...])).T.astype(BF16)
        outs = [jnp.dot(o_lat[h * SMLA_SLOTS:(h + 1) * SMLA_SLOTS], wuv_ref[h], preferred_element_type=F32)
                for h in range(MLA_HEADS)]
        o_ref[0] = jnp.concatenate([jnp.concatenate(outs[2 * j:2 * j + 2], axis=1)
                                    for j in range(MLA_HEADS // 2)], axis=1)


def _smla(layer, pt, cache_t, qf_pad, wuka, wukb, wuv, new_t, t_new):
    b, n_pages = pt.shape
    pps = _pages_per_step(n_pages)
    page = lambda j: pl.BlockSpec(
        (1, 1, MLA_ROW, LANES), lambda bi, s, pt_ref: (pt_ref[bi, s * pps + j], layer, 0, 0))
    const = lambda shape: pl.BlockSpec(shape, lambda bi, s, pt_ref: (0,) * len(shape))
    rows = MLA_HEADS * SMLA_SLOTS
    grid_spec = pltpu.PrefetchScalarGridSpec(
        num_scalar_prefetch=1,
        grid=(b, n_pages // pps),
        in_specs=[page(j) for j in range(pps)] + [
            pl.BlockSpec((1, SMLA_SLOTS, MLA_HEADS * LANES), lambda bi, s, pt_ref: (bi, 0, 0)),
            const(wuka.shape), const(wukb.shape), const(wuv.shape),
            pl.BlockSpec((1, MLA_ROW, LANES), lambda bi, s, pt_ref: (bi, 0, 0))],
        out_specs=pl.BlockSpec((1, SMLA_SLOTS, MLA_HEADS * V_DIM), lambda bi, s, pt_ref: (bi, 0, 0)),
        scratch_shapes=[pltpu.VMEM((rows, KV_LORA), F32), pltpu.VMEM((rows, LANES), F32),
                        pltpu.VMEM((rows, 1), F32), pltpu.VMEM((rows, 1), F32),
                        pltpu.VMEM((MLA_ROW, rows), F32)],
    )
    return pl.pallas_call(
        functools.partial(_smla_kernel, pps, t_new),
        grid_spec=grid_spec,
        out_shape=jax.ShapeDtypeStruct((b, SMLA_SLOTS, MLA_HEADS * V_DIM), F32),
        compiler_params=_params(2),
        name="smla",
    )(pt, *([cache_t] * pps), qf_pad, wuka, wukb, wuv, new_t)


def _sdil_kernel(t_new, q_ref, new_ref, c0_ref, c1_ref, c2_ref, o_ref):
    hw = DIL_HEADS * HEAD_DIM
    rows = DIL_HEADS * 8
    r_idx = lax.broadcasted_iota(jnp.int32, (rows, hw), 0)
    l_idx = lax.broadcasted_iota(jnp.int32, (rows, hw), 1)
    head_mask = jnp.where((r_idx >> 3) == (l_idx >> 6), 1.0, 0.0)
    outs, lses = [], []
    for g, (c_ref, (window, dil)) in enumerate(zip((c0_ref, c1_ref, c2_ref), DIL_PATTERNS)):
        lb = c_ref.shape[3]
        q = q_ref[0, :, g * hw:(g + 1) * hw]
        qs = jnp.concatenate([q] * DIL_HEADS, axis=0) * head_mask
        kc, vc = c_ref[0, 0, 0:hw, :], c_ref[0, 0, hw:2 * hw, :]
        kn = new_ref[0, g * 2 * hw:g * 2 * hw + hw, :]
        vn = new_ref[0, g * 2 * hw + hw:(g + 1) * 2 * hw, :]
        t_c = lax.broadcasted_iota(jnp.int32, (rows, lb), 0) & 7
        d_c = lb + t_c - lax.broadcasted_iota(jnp.int32, (rows, lb), 1)
        mask_c = (d_c >= 0) & ((d_c & (dil - 1)) == 0) & (d_c <= window)
        t_n = lax.broadcasted_iota(jnp.int32, (rows, LANES), 0) & 7
        c_n = lax.broadcasted_iota(jnp.int32, (rows, LANES), 1)
        d_n = t_n - c_n
        mask_n = (c_n < t_new) & (d_n >= 0) & ((d_n & (dil - 1)) == 0) & (d_n <= window)
        (p1, p2), den, m = _softmax_rows([(_dot(qs, kc), mask_c), (_dot(qs, kn), mask_n)])
        o = (_dot_nt(p1, vc) + _dot_nt(p2, vn)) * head_mask
        outs.append(jnp.sum(o.reshape(DIL_HEADS, 8, hw), axis=0))
        lse = (jnp.log(den) + m) * head_mask
        lses.append(jnp.sum(lse.reshape(DIL_HEADS, 8, hw), axis=0))
    mx = jnp.maximum(jnp.maximum(lses[0], lses[1]), lses[2])
    ws = [jnp.exp(x - mx) for x in lses]
    o_ref[0] = (ws[0] * outs[0] + ws[1] * outs[1] + ws[2] * outs[2]) / (ws[0] + ws[1] + ws[2])


def _sdil(layer, q_pad, new_pad, caches, t_new):
    b = q_pad.shape[0]
    hw = DIL_HEADS * HEAD_DIM
    cspec = lambda c: pl.BlockSpec((1, 1, 2 * hw, c.shape[3]), lambda i: (layer, i, 0, 0))
    return pl.pallas_call(
        functools.partial(_sdil_kernel, t_new),
        grid=(b,),
        in_specs=[pl.BlockSpec((1, 8, N_DIL * hw), lambda i: (i, 0, 0)),
                  pl.BlockSpec((1, N_DIL * 2 * hw, LANES), lambda i: (i, 0, 0))] + [cspec(c) for c in caches],
        out_specs=pl.BlockSpec((1, 8, hw), lambda i: (i, 0, 0)),
        out_shape=jax.ShapeDtypeStruct((b, 8, hw), F32),
        compiler_params=_params(1),
        name="sdil",
    )(q_pad, new_pad, *caches)


def _pack_w_ab(w):
    q_a, kv_a, g_a, q_lat, kv_lat, kpe = jnp.split(w, np.cumsum(AB_SPLITS)[:-1].tolist(), axis=1)
    d = w.shape[0]
    qh = q_a.reshape(d, NSA_HEADS, HEAD_DIM)
    zero = jnp.zeros((d, HEAD_DIM), w.dtype)
    chunks = []
    for h in range(NSA_HEADS):
        pair = [qh[:, h], zero] if h // NSA_GROUP == 0 else [zero, qh[:, h]]
        chunks.append(jnp.concatenate(pair, axis=1))
    misc = jnp.concatenate([kpe, g_a, jnp.zeros((d, LANES - QK_ROPE - 3 * NSA_HEADS), w.dtype)], axis=1)
    return jnp.concatenate(chunks + [kv_a, q_lat, kv_lat, misc], axis=1).astype(BF16)


def _pack_w_uq(w):
    wh = w.reshape(Q_LORA, MLA_HEADS, QK_NOPE + QK_ROPE)
    pad = jnp.zeros((Q_LORA, MLA_HEADS, LANES - QK_NOPE - QK_ROPE), w.dtype)
    return jnp.concatenate([wh[..., QK_NOPE:], pad, wh[..., :QK_NOPE]], axis=-1).reshape(
        Q_LORA, MLA_HEADS * LANES).astype(BF16)


def _pack_w_kvup(w_uk, w_uv):
    k_top = jnp.concatenate([jnp.zeros((KV_LORA, MLA_HEADS, LANES - QK_NOPE), w_uk.dtype), w_uk], axis=-1)
    eye = jnp.eye(LANES, dtype=w_uk.dtype) * (np.arange(LANES) < QK_ROPE)[:, None]
    k_bot = jnp.broadcast_to(eye[:, None, :], (LANES, MLA_HEADS, LANES))
    k_part = jnp.concatenate([k_top, k_bot], axis=0).reshape(KV_LORA + LANES, MLA_HEADS * LANES)
    v_part = jnp.concatenate([w_uv.reshape(KV_LORA, MLA_HEADS * V_DIM),
                              jnp.zeros((LANES, MLA_HEADS * V_DIM), w_uv.dtype)], axis=0)
    return jnp.concatenate([k_part, v_part], axis=1).astype(BF16)


def _pack_w_absorb(w_uk, w_uv):
    zero = jnp.zeros((MLA_HEADS, LANES - QK_NOPE, KV_LORA), w_uk.dtype)
    up_a = jnp.concatenate([zero, jnp.transpose(w_uk, (1, 2, 0))], axis=1)
    place = np.zeros((LANES, LANES), np.float32)
    place[np.arange(QK_ROPE), LANES - QK_ROPE + np.arange(QK_ROPE)] = 1.0
    up_b = jnp.broadcast_to(jnp.asarray(place)[None], (MLA_HEADS, LANES, LANES))
    down = jnp.transpose(w_uv, (1, 0, 2))
    return up_a.astype(BF16), up_b.astype(BF16), down.astype(BF16)


def _block_diag2(phi):
    z = jnp.zeros((HEAD_DIM, HEAD_DIM), phi.dtype)
    return jnp.concatenate([jnp.concatenate([phi[0], z], axis=1),
                            jnp.concatenate([z, phi[1]], axis=1)], axis=0).astype(BF16)


def _pad_rows(x, rows):
    return jnp.pad(x, ((0, 0), (0, rows - x.shape[1]), (0, 0)))


def _stack_group_rows(q, b, t):
    q5 = q.reshape(b, t, NSA_KV_HEADS, NSA_GROUP, LANES)
    q5 = jnp.pad(jnp.transpose(q5, (0, 2, 3, 1, 4)), ((0, 0), (0, 0), (0, 0), (0, 8 - t), (0, 0)))
    return q5.reshape(b, NSA_KV_HEADS, NSA_GROUP * 8, LANES)


def kernel(x_prompt, x_sample, cache_nsa_kv, cache_mla, cache_nsa_win, cache_dil0, cache_dil1, cache_dil2,
           page_table, p_prompt, p_sample, w_in_ab, w_out_ab, nsa_pe_k, nsa_pe_v, nsa_phi_k, nsa_phi_v,
           mla_q_norm, mla_kv_norm, mla_w_uq, mla_w_uk, mla_w_uv, w_in_c, w_out_c, norm_mix, norm_ffn,
           w_gate_up, w_down, norm_ple, w_ple_gate, w_ple_proj, norm_final):
    bp, tp, _ = x_prompt.shape
    bs, ts, _ = x_sample.shape
    depth = norm_mix.shape[0]
    n_ab = w_in_ab.shape[0]
    n_pool, page_size = cache_nsa_kv.shape[:2]
    n_pages = page_table.shape[1]
    past_len = n_pages * page_size
    assert page_size == LANES and tp % (2 * SEL_BLOCK) == 0 and ts <= 8 and past_len % SEL_BLOCK == 0
    assert MLA_TQ == MLA_TK
    np_, ns_ = bp * tp, bs * ts
    hw = DIL_HEADS * HEAD_DIM

    pos_p = jnp.arange(tp, dtype=jnp.int32)
    pos_s = past_len + jnp.arange(ns_, dtype=jnp.int32) % ts
    tile_p = min(ROW_TILE, tp)
    tile_s = ns_
    tabs = {
        "p": (_rope_tables(pos_p, ROT_DIM, HEAD_DIM, 0), _rope_tables(pos_p, QK_ROPE, LANES, 0), tile_p),
        "s": (_rope_tables(pos_s, ROT_DIM, HEAD_DIM, 0), _rope_tables(pos_s, QK_ROPE, LANES, 0), tile_s),
    }
    expand = jnp.asarray(np.arange(tp)[:, None] // SEL_BLOCK == np.arange(tp // SEL_BLOCK)[None], BF16)

    def rows_last(c, feat):
        perm = tuple(i for i in range(c.ndim) if i != feat) + (feat,)
        return jnp.transpose(c, perm)

    cache_nsa_t = rows_last(cache_nsa_kv, 1).reshape(n_pool, n_ab, 512, page_size)
    cache_mla_t = rows_last(cache_mla, 1)
    cache_win_t = rows_last(cache_nsa_win, 2).reshape(n_ab, bs, 256, -1)
    caches_dil_t = [rows_last(c, 2).reshape(c.shape[0], bs, 2 * hw, -1) for c in (cache_dil0, cache_dil1, cache_dil2)]
    slots_last = lambda x: jnp.transpose(x, (0, 2, 1))

    row2 = lambda v: v.reshape(1, -1)
    h_p = x_prompt.reshape(np_, D_MODEL)
    h_s = x_sample.reshape(ns_, D_MODEL)
    nsa_p, nsa_s, mla_p, mla_s, win_p, win_s = [], [], [], [], [], []
    dil_p = [[] for _ in DIL_PATTERNS]
    dil_s = [[] for _ in DIL_PATTERNS]
    y_p = y_s = None

    for i in range(depth):
        l = i // 2
        final = i == depth - 1
        gm = row2(norm_mix[i])
        if i % 2 == 0:
            w_ab = _pack_w_ab(w_in_ab[l])
            w_uq = _pack_w_uq(mla_w_uq[l])
            gq, gkv = row2(mla_q_norm[l]), row2(mla_kv_norm[l])
            pe = jnp.concatenate([nsa_pe_k[l].reshape(CMP_BLOCK, 128), nsa_pe_v[l].reshape(CMP_BLOCK, 128)], axis=1)
            phik, phiv = _block_diag2(nsa_phi_k[l]), _block_diag2(nsa_phi_v[l])
            w_out = w_out_ab[l].astype(BF16)

            t64, tm_, tile = tabs["p"]
            qc, qr, rows, win, qf, mla, misc = _pre_ab(h_p, gm, w_ab, gq, gkv, w_uq, t64, tm_, tile)
            rows3 = rows.reshape(bp, tp, 512)
            kc, vc = _cmp_prompt(rows3, pe, phik, phiv)
            o_a = _nsa_prompt(qc.reshape(bp, tp, -1), qr.reshape(bp, tp, -1), rows3, win.reshape(bp, tp, 256),
                              kc, vc, misc.reshape(bp, tp, LANES), expand)
            kf, vv = _mla_kv(mla, misc, _pack_w_kvup(mla_w_uk[l], mla_w_uv[l]), tile)
            o_b = _mla_prompt(qf.reshape(bp, tp, -1), kf.reshape(bp, tp, -1), vv.reshape(bp, tp, -1))
            mix_p = [o_a.reshape(np_, -1), o_b.reshape(np_, -1)]
            nsa_p.append(rows3)
            mla_p.append(mla.reshape(bp, tp, MLA_ROW))
            win_p.append(win.reshape(bp, tp, 256)[:, max(tp - NSA_WINDOW, 0):])

            t64, tm_, tile = tabs["s"]
            qc, qr, rows, win, qf, mla, misc = _pre_ab(h_s, gm, w_ab, gq, gkv, w_uq, t64, tm_, tile)
            o_cmp, sel_idx = _snsa_a(l, page_table, cache_nsa_t, _stack_group_rows(qc, bs, ts),
                                     pe, phik, phiv, ts)
            sidx = jnp.transpose(sel_idx[:, :, :ts, :N_SEL], (0, 2, 1, 3)).reshape(-1)
            q_tg = jnp.pad(qr.reshape(bs, ts, NSA_KV_HEADS, NSA_GROUP, LANES),
                           ((0, 0), (0, 0), (0, 0), (0, 8 - NSA_GROUP), (0, 0)))
            new_blk = slots_last(_pad_rows(rows.reshape(bs, ts, 512)[:, :, 256:512], LANES))
            o_slc = _snsa_b(l, page_table, sidx, cache_nsa_t, q_tg, new_blk, ts)
            o_win = _snsa_win(l, _stack_group_rows(qr, bs, ts), cache_win_t,
                              slots_last(_pad_rows(win.reshape(bs, ts, 256), LANES)), past_len, ts)
            o_a = _scomb(o_cmp, o_slc, o_win, _pad_rows(misc.reshape(bs, ts, LANES), 8), ts)
            o_a = o_a[:, :ts].reshape(ns_, NSA_HEADS * HEAD_DIM)
            mla_new = slots_last(_pad_rows(mla.reshape(bs, ts, MLA_ROW), LANES))
            o_b = _smla(l, page_table, cache_mla_t, _pad_rows(qf.reshape(bs, ts, -1), SMLA_SLOTS),
                        *_pack_w_absorb(mla_w_uk[l], mla_w_uv[l]), mla_new, ts)
            mix_s = [o_a, o_b[:, :ts].reshape(ns_, -1)]
            nsa_s.append(rows.reshape(bs, ts, 512))
            mla_s.append(mla.reshape(bs, ts, MLA_ROW))
            win_s.append(win.reshape(bs, ts, 256))
        else:
            w_c = w_in_c[l].astype(BF16)
            w_out = w_out_c[l].astype(BF16)

            t64, _, tile = tabs["p"]
            outs = _pre_c(h_p, gm, w_c, t64, tile)
            os_, ls_ = [], []
            for g, (window, dil) in enumerate(DIL_PATTERNS):
                o, lse = _dil_prompt(outs[g], outs[N_DIL + g], bp, tp, dil)
                os_.append(o)
                ls_.append(lse)
                dil_p[g].append(outs[N_DIL + g].reshape(bp, tp, 2 * hw)[:, max(tp - window, 0):])
            mix_p = [_dil_merge(os_, ls_, tile)]

            t64, _, tile = tabs["s"]
            outs = _pre_c(h_s, gm, w_c, t64, tile)
            q_pad = _pad_rows(jnp.concatenate(outs[:N_DIL], axis=1).reshape(bs, ts, -1), 8)
            new_pad = slots_last(_pad_rows(jnp.concatenate(outs[N_DIL:], axis=1).reshape(bs, ts, -1), LANES))
            mix_s = [_sdil(l, q_pad, new_pad, caches_dil_t, ts)[:, :ts].reshape(ns_, hw)]
            for g in range(N_DIL):
                dil_s[g].append(outs[N_DIL + g].reshape(bs, ts, 2 * hw))

        post_w = (w_out, row2(norm_ffn[i]), w_gate_up[i].astype(BF16), w_down[i].astype(BF16),
                  row2(norm_ple[i]), w_ple_gate[i].astype(BF16))
        wpp, gfin = w_ple_proj[i].astype(BF16), row2(norm_final)
        res = _post(h_p, mix_p, *post_w, p_prompt[i].reshape(np_, PLE_DIM), wpp, gfin, final, min(POST_TILE, tp))
        h_p = res[0]
        if final:
            y_p = res[1]
        res = _post(h_s, mix_s, *post_w, p_sample[i].reshape(ns_, PLE_DIM), wpp, gfin, final, tabs["s"][2])
        h_s = res[0]
        if final:
            y_s = res[1]

    def rows_out(parts, b, t, tail):
        return jnp.stack(parts, axis=2).reshape((b, t, len(parts)) + tail)

    def bufs_out(parts, tail):
        x = jnp.stack(parts, axis=0)
        return x.reshape(x.shape[:3] + tail)

    kv_tail = (4, NSA_KV_HEADS, HEAD_DIM)
    win_tail = (2, NSA_KV_HEADS, HEAD_DIM)
    dil_tail = (2, DIL_HEADS, HEAD_DIM)
    out = [y_p.reshape(bp, tp, D_MODEL), y_s.reshape(bs, ts, D_MODEL),
           rows_out(nsa_p, bp, tp, kv_tail), rows_out(nsa_s, bs, ts, kv_tail),
           rows_out(mla_p, bp, tp, (MLA_ROW,)), rows_out(mla_s, bs, ts, (MLA_ROW,)),
           bufs_out(win_p, win_tail), bufs_out(win_s, win_tail)]
    for g in range(N_DIL):
        out += [bufs_out(dil_p[g], dil_tail), bufs_out(dil_s[g], dil_tail)]
    return tuple(out)
```

```python
import functools

import numpy as np
import jax
import jax.numpy as jnp
from jax import lax
from jax.experimental import pallas as pl
from jax.experimental.pallas import tpu as pltpu

F32 = jnp.float32
BF16 = jnp.bfloat16

D_MODEL = 1024
HEAD_DIM = 64
ROT_DIM = HEAD_DIM // 4
ROPE_THETA = 500000.0
RMS_EPS = 1e-6
NSA_HEADS = 8
NSA_KV_HEADS = 2
NSA_GROUP = NSA_HEADS // NSA_KV_HEADS
CMP_BLOCK = 32
SEL_BLOCK = 64
N_SEL = 16
NSA_WINDOW = 512
MLA_HEADS = 8
Q_LORA = 384
KV_LORA = 256
QK_NOPE = 64
QK_ROPE = 32
V_DIM = 64
MLA_ROW = KV_LORA + QK_ROPE
DIL_PATTERNS = ((128, 1), (512, 4), (2048, 16))
N_DIL = len(DIL_PATTERNS)
DIL_HEADS = 8
DIL_BACK = 128
D_FF = ((-(-8 * D_MODEL // 3) + 255) // 256) * 256
PLE_DIM = 256
AB_SPLITS = (NSA_HEADS * HEAD_DIM, 6 * NSA_KV_HEADS * HEAD_DIM, 3 * NSA_HEADS, Q_LORA, KV_LORA, QK_ROPE)

LANES = 128
NEG = -1e30
V7X_VMEM_LIMIT = 56 * 1024 * 1024
ROW_TILE = 256
FF_CHUNK = 256
POST_TILE = 512
NSA_TQ = 512
NSA_TK = 512
MLA_TQ = 1024
MLA_TK = 1024
PAGES_PER_STEP = 32
SMLA_SLOTS = 16


def _dot(a, b):
    return jnp.dot(a.astype(BF16), b.astype(BF16), preferred_element_type=F32)


def _dot_nt(a, b):
    return lax.dot_general(a.astype(BF16), b.astype(BF16), (((1,), (1,)), ((), ())),
                           preferred_element_type=F32)


def _rms(x, g):
    return x * lax.rsqrt(jnp.mean(x * x, axis=-1, keepdims=True) + RMS_EPS) * g


def _tile_lanes(a, reps):
    return a if reps == 1 else jnp.concatenate([a] * reps, axis=1)


def _rope(x, cos, sin_up, sin_dn, half):
    w = x.shape[-1]
    reps = w // LANES
    return (x * _tile_lanes(cos, reps)
            + pltpu.roll(x, half, 1) * _tile_lanes(sin_up, reps)
            + pltpu.roll(x, w - half, 1) * _tile_lanes(sin_dn, reps))


def _rope_tables(pos, rot_dim, period, offset):
    half = rot_dim // 2
    inv = jnp.power(ROPE_THETA, -jnp.arange(half, dtype=F32) / half)
    ang = pos.astype(F32)[:, None] * inv[None]
    cos, sin = jnp.cos(ang), jnp.sin(ang)
    lane = np.arange(LANES) % period - offset
    lo = (lane >= 0) & (lane < half)
    hi = (lane >= half) & (lane < rot_dim)
    j = np.clip(np.where(hi, lane - half, lane), 0, half - 1)
    cos_t = jnp.where(lo | hi, cos[:, j], 1.0)
    sin_up = jnp.where(hi, sin[:, j], 0.0)
    sin_dn = jnp.where(lo, -sin[:, j], 0.0)
    return cos_t, sin_up, sin_dn


def _softmax_cols(pieces):
    m = None
    for s, mask in pieces:
        mm = jnp.max(jnp.where(mask, s, NEG), axis=0, keepdims=True)
        m = mm if m is None else jnp.maximum(m, mm)
    es = [jnp.where(mask, jnp.exp(s - m), 0.0) for s, mask in pieces]
    den = sum(jnp.sum(e, axis=0, keepdims=True) for e in es)
    inv = 1.0 / jnp.maximum(den, 1e-30)
    return [e * inv for e in es], den, m


def _softmax_rows(pieces):
    m = None
    for s, mask in pieces:
        mm = jnp.max(jnp.where(mask, s, NEG), axis=1, keepdims=True)
        m = mm if m is None else jnp.maximum(m, mm)
    es = [jnp.where(mask, jnp.exp(s - m), 0.0) for s, mask in pieces]
    den = sum(jnp.sum(e, axis=1, keepdims=True) for e in es)
    inv = 1.0 / jnp.maximum(den, 1e-30)
    return [e * inv for e in es], den, m


def _flash_update(carry, s_t, mask, v_t):
    m, l, acc = carry
    if mask is not None:
        s_t = jnp.where(mask, s_t, NEG)
    m_new = jnp.maximum(m, jnp.max(s_t, axis=0, keepdims=True))
    p = jnp.exp(s_t - m_new)
    alpha = jnp.exp(m - m_new)
    l_new = alpha * l + jnp.sum(p, axis=0, keepdims=True)
    acc_new = alpha * acc + jnp.dot(v_t, p.astype(BF16), preferred_element_type=F32)
    return m_new, l_new, acc_new


def _params(n_grid):
    return pltpu.CompilerParams(dimension_semantics=("arbitrary",) * n_grid,
                                vmem_limit_bytes=V7X_VMEM_LIMIT)


def _resident(shape):
    nd = len(shape)
    return pl.BlockSpec(shape, lambda *_: (0,) * nd, pipeline_mode=pl.Buffered(1))


def _rows(width, tile):
    return pl.BlockSpec((tile, width), lambda i: (i, 0))


def _pages_per_step(n_pages):
    pps = min(PAGES_PER_STEP, n_pages)
    assert n_pages % pps == 0 and pps % 4 == 0
    return pps


def _pre_ab_kernel(h_ref, gm_ref, w_ref, gq_ref, gkv_ref, wuq_ref,
                   c64_ref, u64_ref, d64_ref, cm_ref, um_ref, dm_ref,
                   qc_ref, qr_ref, rows_ref, win_ref, qf_ref, mla_ref, misc_ref):
    t64 = (c64_ref[...], u64_ref[...], d64_ref[...])
    tm_ = (cm_ref[...], um_ref[...], dm_ref[...])
    hn = _rms(h_ref[...], gm_ref[...])
    z = _dot(hn, w_ref[...])
    nq = NSA_HEADS * LANES
    qc = z[:, 0:nq] * (HEAD_DIM ** -0.5)
    qc_ref[...] = qc
    qr_ref[...] = _rope(qc, *t64, ROT_DIM // 2)
    kv = z[:, nq:nq + 768]
    rows_ref[:, 0:256] = kv[:, 0:256]
    rows_ref[:, 256:384] = _rope(kv[:, 256:384], *t64, ROT_DIM // 2)
    rows_ref[:, 384:512] = kv[:, 384:512]
    win_ref[:, 0:128] = _rope(kv[:, 512:640], *t64, ROT_DIM // 2)
    win_ref[:, 128:256] = kv[:, 640:768]
    o = nq + 768
    qn = _rms(z[:, o:o + Q_LORA], gq_ref[...])
    qf_ref[...] = _rope(_dot(qn, wuq_ref[...]), *tm_, QK_ROPE // 2)
    o += Q_LORA
    ckv = _rms(z[:, o:o + KV_LORA], gkv_ref[...])
    o += KV_LORA
    misc = _rope(z[:, o:o + LANES], *tm_, QK_ROPE // 2)
    mla_ref[:, 0:KV_LORA] = ckv
    mla_ref[:, KV_LORA:MLA_ROW] = misc[:, 0:QK_ROPE]
    misc_ref[...] = misc


def _pre_ab(h, gm, w, gq, gkv, wuq, tabs64, tabsm, tile):
    n = h.shape[0]
    t_tab = tabs64[0].shape[0]
    nt = t_tab // tile
    tab = pl.BlockSpec((tile, LANES), lambda i: (i % nt, 0))
    widths = (NSA_HEADS * LANES, NSA_HEADS * LANES, 512, 256, MLA_HEADS * LANES, MLA_ROW, LANES)
    return pl.pallas_call(
        _pre_ab_kernel,
        grid=(n // tile,),
        in_specs=[_rows(D_MODEL, tile), _resident(gm.shape), _resident(w.shape), _resident(gq.shape),
                  _resident(gkv.shape), _resident(wuq.shape)] + [tab] * 6,
        out_specs=[_rows(wd, tile) for wd in widths],
        out_shape=[jax.ShapeDtypeStruct((n, wd), F32) for wd in widths],
        compiler_params=_params(1),
        name="pre_ab",
    )(h, gm, w, gq, gkv, wuq, *tabs64, *tabsm)


def _mla_kv_kernel(mla_ref, misc_ref, w_ref, k_ref, v_ref):
    x = jnp.concatenate([mla_ref[:, 0:KV_LORA], misc_ref[...]], axis=1)
    z = _dot(x, w_ref[...])
    k_ref[...] = z[:, 0:MLA_HEADS * LANES]
    v_ref[...] = z[:, MLA_HEADS * LANES:]


def _mla_kv(mla, misc, w, tile):
    n = mla.shape[0]
    return pl.pallas_call(
        _mla_kv_kernel,
        grid=(n // tile,),
        in_specs=[_rows(MLA_ROW, tile), _rows(LANES, tile), _resident(w.shape)],
        out_specs=[_rows(MLA_HEADS * LANES, tile), _rows(MLA_HEADS * V_DIM, tile)],
        out_shape=[jax.ShapeDtypeStruct((n, MLA_HEADS * LANES), F32),
                   jax.ShapeDtypeStruct((n, MLA_HEADS * V_DIM), F32)],
        compiler_params=_params(1),
        name="mla_kv",
    )(mla, misc, w)


def _pre_c_kernel(h_ref, gm_ref, w_ref, c64_ref, u64_ref, d64_ref, *out_refs):
    t64 = (c64_ref[...], u64_ref[...], d64_ref[...])
    hn = _rms(h_ref[...], gm_ref[...])
    z = _dot(hn, w_ref[...])
    hw = DIL_HEADS * HEAD_DIM
    for g in range(N_DIL):
        o = g * 3 * hw
        out_refs[g][...] = _rope(z[:, o:o + hw] * (HEAD_DIM ** -0.5), *t64, ROT_DIM // 2)
        out_refs[N_DIL + g][:, 0:hw] = _rope(z[:, o + hw:o + 2 * hw], *t64, ROT_DIM // 2)
        out_refs[N_DIL + g][:, hw:2 * hw] = z[:, o + 2 * hw:o + 3 * hw]


def _pre_c(h, gm, w, tabs64, tile):
    n = h.shape[0]
    nt = tabs64[0].shape[0] // tile
    tab = pl.BlockSpec((tile, LANES), lambda i: (i % nt, 0))
    hw = DIL_HEADS * HEAD_DIM
    widths = (hw,) * N_DIL + (2 * hw,) * N_DIL
    return pl.pallas_call(
        _pre_c_kernel,
        grid=(n // tile,),
        in_specs=[_rows(D_MODEL, tile), _resident(gm.shape), _resident(w.shape)] + [tab] * 3,
        out_specs=[_rows(wd, tile) for wd in widths],
        out_shape=[jax.ShapeDtypeStruct((n, wd), F32) for wd in widths],
        compiler_params=_params(1),
        name="pre_c",
    )(h, gm, w, *tabs64)


def _post_kernel(final, n_mix, h_ref, *refs):
    a_refs = refs[:n_mix]
    (wout_ref, gffn_ref, wgu_ref, wd_ref, gple_ref, wpg_ref, ple_ref, wpp_ref, gfin_ref, o_ref) = refs[n_mix:n_mix + 10]
    y_ref = refs[n_mix + 10:]
    h1 = h_ref[...]
    off = 0
    for a_ref in a_refs:
        h1 = h1 + _dot(a_ref[...], wout_ref[off:off + a_ref.shape[1], :])
        off += a_ref.shape[1]
    xn = _rms(h1, gffn_ref[...]).astype(BF16)
    acc = jnp.zeros(h1.shape, F32)
    for c in range(D_FF // FF_CHUNK):
        lo = c * FF_CHUNK
        g = jnp.dot(xn, wgu_ref[:, lo:lo + FF_CHUNK], preferred_element_type=F32)
        u = jnp.dot(xn, wgu_ref[:, D_FF + lo:D_FF + lo + FF_CHUNK], preferred_element_type=F32)
        a = g * jax.nn.sigmoid(g) * u
        acc = acc + jnp.dot(a.astype(BF16), wd_ref[lo:lo + FF_CHUNK, :], preferred_element_type=F32)
    h2 = h1 + acc
    gate = jax.nn.sigmoid(_dot(_rms(h2, gple_ref[...]), wpg_ref[...]))
    h3 = h2 + gate * _dot(ple_ref[...], wpp_ref[...])
    o_ref[...] = h3
    if final:
        y_ref[0][...] = _rms(h3, gfin_ref[...])


def _post(h, mix, wout, gffn, wgu, wd, gple, wpg, ple, wpp, gfin, final, tile):
    n = h.shape[0]
    n_out = 2 if final else 1
    return pl.pallas_call(
        functools.partial(_post_kernel, final, len(mix)),
        grid=(n // tile,),
        in_specs=[_rows(D_MODEL, tile)] + [_rows(a.shape[1], tile) for a in mix] + [
                  _resident(wout.shape), _resident(gffn.shape),
                  _resident(wgu.shape), _resident(wd.shape), _resident(gple.shape), _resident(wpg.shape),
                  _rows(PLE_DIM, tile), _resident(wpp.shape), _resident(gfin.shape)],
        out_specs=[_rows(D_MODEL, tile)] * n_out,
        out_shape=[jax.ShapeDtypeStruct((n, D_MODEL), F32)] * n_out,
        compiler_params=_params(1),
        name="post",
    )(h, *mix, wout, gffn, wgu, wd, gple, wpg, ple, wpp, gfin)


def _cmp_kernel(rows_ref, pe_ref, phik_ref, phiv_ref, kc_ref, vc_ref):
    t = rows_ref.shape[1]
    nsb = t // SEL_BLOCK
    x3 = rows_ref[0, :, 0:256].reshape(nsb, SEL_BLOCK, 256)
    pe = pe_ref[...][None]
    even = jnp.mean(x3[:, 0:CMP_BLOCK, :] + pe, axis=1)
    odd = jnp.mean(x3[:, CMP_BLOCK:SEL_BLOCK, :] + pe, axis=1)
    km = jnp.concatenate([even, odd], axis=0)
    kc_ref[0] = _dot(km[:, 0:128], phik_ref[...])
    vc_ref[0] = _dot(km[:, 128:256], phiv_ref[...])


def _cmp_prompt(rows3, pe, phik, phiv):
    b, t, _ = rows3.shape
    ncb = t // CMP_BLOCK
    return pl.pallas_call(
        _cmp_kernel,
        grid=(b,),
        in_specs=[pl.BlockSpec((1, t, 512), lambda i: (i, 0, 0)), _resident(pe.shape),
                  _resident(phik.shape), _resident(phiv.shape)],
        out_specs=[pl.BlockSpec((1, ncb, LANES), lambda i: (i, 0, 0))] * 2,
        out_shape=[jax.ShapeDtypeStruct((b, ncb, LANES), F32)] * 2,
        compiler_params=_params(1),
        name="nsa_cmp",
    )(rows3, pe, phik, phiv)


def _nsa_prompt_kernel(qc_ref, qr_ref, rows_ref, win_ref, kc_ref, vc_ref, misc_ref, exp_ref, o_ref):
    qi = pl.program_id(1)
    tq = qc_ref.shape[1]
    t_len = rows_ref.shape[1]
    nsb = t_len // SEL_BLOCK
    ncb = 2 * nsb
    q0 = qi * tq
    gates_t = jax.nn.sigmoid(misc_ref[0].T)
    kc = kc_ref[0].astype(BF16)
    vc_t = vc_ref[0].T.astype(BF16)
    qpos1 = q0 + lax.broadcasted_iota(jnp.int32, (1, tq), 1)
    qpos4 = _tile_lanes(qpos1, NSA_GROUP)
    out_rows = []
    for g in range(NSA_KV_HEADS):
        heads = range(g * NSA_GROUP, (g + 1) * NSA_GROUP)
        qc_g = jnp.concatenate([qc_ref[0, :, h * LANES:(h + 1) * LANES] for h in heads], axis=0).astype(BF16)
        qr_g = jnp.concatenate([qr_ref[0, :, h * LANES:(h + 1) * LANES] for h in heads], axis=0).astype(BF16)
        s_c = _dot_nt(kc, qc_g)
        n_idx = lax.broadcasted_iota(jnp.int32, (ncb, NSA_GROUP * tq), 0)
        n_orig = jnp.where(n_idx < nsb, 2 * n_idx, 2 * (n_idx - nsb) + 1)
        cmask = (n_orig + 1) * CMP_BLOCK - 1 <= qpos4
        (p_c,), _, _ = _softmax_cols([(s_c, cmask)])
        o_c = jnp.dot(vc_t, p_c.astype(BF16), preferred_element_type=F32)[g * HEAD_DIM:(g + 1) * HEAD_DIM]
        imp = sum(p_c[:, r * tq:(r + 1) * tq] for r in range(NSA_GROUP))
        imp = imp[0:nsb] + imp[nsb:ncb]
        blk = lax.broadcasted_iota(jnp.int32, (nsb, tq), 0)
        forced = (blk == 0) | (blk == (qpos1 >> 6))
        val = jnp.where(forced, jnp.inf, jnp.where(blk * SEL_BLOCK <= qpos1, imp, -jnp.inf))
        rank = jnp.zeros((nsb, tq), F32)
        for i in range(nsb):
            vi = val[i:i + 1, :]
            rank = rank + jnp.where(vi > val, 1.0, 0.0) + jnp.where(vi == val, jnp.where(blk > i, 1.0, 0.0), 0.0)
        sel = jnp.where(rank < min(N_SEL, nsb), jnp.where(val > -jnp.inf, 1.0, 0.0), 0.0).astype(BF16)

        init = (jnp.full((1, NSA_GROUP * tq), NEG, F32), jnp.zeros((1, NSA_GROUP * tq), F32),
                jnp.zeros((HEAD_DIM, NSA_GROUP * tq), F32))

        def slc_body(kt, carry):
            k0 = pl.multiple_of(kt * NSA_TK, NSA_TK)
            k = rows_ref[0, pl.ds(k0, NSA_TK), 256:384]
            v = rows_ref[0, pl.ds(k0, NSA_TK), 384:512]
            s_t = _dot_nt(k, qr_g)
            chosen = jnp.dot(exp_ref[pl.ds(k0, NSA_TK), :], sel, preferred_element_type=F32)
            kpos = k0 + lax.broadcasted_iota(jnp.int32, (NSA_TK, tq), 0)
            mask = _tile_lanes((chosen > 0.5) & (kpos <= qpos1), NSA_GROUP)
            v_t = v.T[g * HEAD_DIM:(g + 1) * HEAD_DIM].astype(BF16)
            return _flash_update(carry, s_t, mask, v_t)

        n_kt = (q0 + tq - 1) // NSA_TK + 1
        _, l_s, acc_s = lax.fori_loop(0, n_kt, slc_body, init)
        o_s = acc_s / l_s

        carry = init
        n_back_tiles = -(-(NSA_WINDOW - 1) // tq)
        for j in range(n_back_tiles + 1):
            kt = qi - j
            k0 = pl.multiple_of(jnp.maximum(kt, 0) * tq, tq)
            k = win_ref[0, pl.ds(k0, tq), 0:128]
            v = win_ref[0, pl.ds(k0, tq), 128:256]
            s_t = _dot_nt(k, qr_g)
            kpos = kt * tq + lax.broadcasted_iota(jnp.int32, (tq, tq), 0)
            mask = _tile_lanes((kpos <= qpos1) & (kpos >= qpos1 - (NSA_WINDOW - 1)) & (kpos >= 0), NSA_GROUP)
            v_t = v.T[g * HEAD_DIM:(g + 1) * HEAD_DIM].astype(BF16)
            carry = _flash_update(carry, s_t, mask, v_t)
        o_w = carry[2] / carry[1]

        for r, h in enumerate(heads):
            sl = slice(r * tq, (r + 1) * tq)
            row = 32 + h
            out_rows.append(gates_t[row:row + 1] * o_c[:, sl]
                            + gates_t[row + 8:row + 9] * o_s[:, sl]
                            + gates_t[row + 16:row + 17] * o_w[:, sl])
    o_ref[0] = jnp.concatenate(out_rows, axis=0).T


def _nsa_prompt(qc3, qr3, rows3, win3, kc, vc, misc3, expand):
    b, t, _ = rows3.shape
    tq = min(NSA_TQ, t)
    ncb = kc.shape[1]
    qspec = pl.BlockSpec((1, tq, NSA_HEADS * LANES), lambda i, j: (i, j, 0))
    return pl.pallas_call(
        _nsa_prompt_kernel,
        grid=(b, t // tq),
        in_specs=[qspec, qspec,
                  pl.BlockSpec((1, t, 512), lambda i, j: (i, 0, 0)),
                  pl.BlockSpec((1, t, 256), lambda i, j: (i, 0, 0)),
                  pl.BlockSpec((1, ncb, LANES), lambda i, j: (i, 0, 0)),
                  pl.BlockSpec((1, ncb, LANES), lambda i, j: (i, 0, 0)),
                  pl.BlockSpec((1, tq, LANES), lambda i, j: (i, j, 0)),
                  _resident(expand.shape)],
        out_specs=pl.BlockSpec((1, tq, NSA_HEADS * HEAD_DIM), lambda i, j: (i, j, 0)),
        out_shape=jax.ShapeDtypeStruct((b, t, NSA_HEADS * HEAD_DIM), F32),
        compiler_params=_params(2),
        name="nsa_prompt",
    )(qc3, qr3, rows3, win3, kc, vc, misc3, expand)


def _mla_prompt_kernel(q_ref, k_ref, v_ref, o_ref):
    qi = pl.program_id(2)
    tq = q_ref.shape[1]
    tk = min(MLA_TK, k_ref.shape[1])
    q0 = qi * tq
    qpos = q0 + lax.broadcasted_iota(jnp.int32, (1, tq), 1)
    scale = (QK_NOPE + QK_ROPE) ** -0.5
    outs = []
    n_kt = (q0 + tq - 1) // tk + 1
    for hh in range(2):
        q = (q_ref[0, :, hh * LANES:(hh + 1) * LANES] * scale).astype(BF16)

        def tile(kt, carry, causal):
            k0 = pl.multiple_of(kt * tk, tk)
            k = k_ref[0, pl.ds(k0, tk), hh * LANES:(hh + 1) * LANES]
            s_t = _dot_nt(k, q)
            mask = (k0 + lax.broadcasted_iota(jnp.int32, (tk, tq), 0) <= qpos) if causal else None
            v_t = v_ref[0, pl.ds(k0, tk), :].T[hh * V_DIM:(hh + 1) * V_DIM].astype(BF16)
            return _flash_update(carry, s_t, mask, v_t)

        init = (jnp.full((1, tq), NEG, F32), jnp.zeros((1, tq), F32), jnp.zeros((V_DIM, tq), F32))
        carry = lax.fori_loop(0, n_kt - 1, lambda kt, c: tile(kt, c, False), init)
        _, l, acc = tile(n_kt - 1, carry, True)
        outs.append(acc / l)
    o_ref[0] = jnp.concatenate(outs, axis=0).T


def _mla_prompt(qf3, kf3, v3):
    b, t, _ = qf3.shape
    tq = min(MLA_TQ, t)
    return pl.pallas_call(
        _mla_prompt_kernel,
        grid=(b, MLA_HEADS // 2, t // tq),
        in_specs=[pl.BlockSpec((1, tq, 2 * LANES), lambda i, p, j: (i, j, p)),
                  pl.BlockSpec((1, t, 2 * LANES), lambda i, p, j: (i, 0, p)),
                  pl.BlockSpec((1, t, 2 * V_DIM), lambda i, p, j: (i, 0, p))],
        out_specs=pl.BlockSpec((1, tq, 2 * V_DIM), lambda i, p, j: (i, j, p)),
        out_shape=jax.ShapeDtypeStruct((b, t, MLA_HEADS * V_DIM), F32),
        compiler_params=_params(3),
        name="mla_prompt",
    )(qf3, kf3, v3)


def _dil_prompt_kernel(sub, has_prev, q_ref, kc_ref, vc_ref, kp_ref, vp_ref, o_ref, lse_ref):
    i = pl.program_id(2)
    tq = q_ref.shape[1]
    lane = lax.broadcasted_iota(jnp.int32, (1, LANES), 1)
    half_masks = [jnp.where(lane < HEAD_DIM, 1.0, 0.0), jnp.where(lane >= HEAD_DIM, 1.0, 0.0)]
    n_keys = 2 * sub if has_prev else sub
    c_idx = lax.broadcasted_iota(jnp.int32, (n_keys, sub), 0)
    q_idx = lax.broadcasted_iota(jnp.int32, (n_keys, sub), 1)
    kpos = c_idx - sub if has_prev else c_idx
    band = (kpos <= q_idx) & (kpos >= q_idx - DIL_BACK)
    for s in range(tq // sub):
        rs = slice(s * sub, (s + 1) * sub)
        q = q_ref[0, rs, :]
        k = kc_ref[0, rs, :]
        v = vc_ref[0, rs, :]
        mask = band
        if has_prev:
            if s == 0:
                k = jnp.concatenate([kp_ref[0], k], axis=0)
                v = jnp.concatenate([vp_ref[0], v], axis=0)
                mask = band & (c_idx >= jnp.where(i > 0, 0, sub))
            else:
                ps = slice((s - 1) * sub, s * sub)
                k = jnp.concatenate([kc_ref[0, ps, :], k], axis=0)
                v = jnp.concatenate([vc_ref[0, ps, :], v], axis=0)
        o_rows, lse_rows = [], []
        mask2 = _tile_lanes(mask, 2)
        for j in range(DIL_HEADS // 2):
            cs = slice(j * LANES, (j + 1) * LANES)
            v_t = v[:, cs].T.astype(BF16)
            q2 = jnp.concatenate([q[:, cs] * half_masks[0], q[:, cs] * half_masks[1]], axis=0)
            s_t = _dot_nt(k[:, cs], q2)
            (p,), den, m = _softmax_cols([(s_t, mask2)])
            o2 = jnp.dot(v_t, p.astype(BF16), preferred_element_type=F32)
            lse2 = jnp.log(den) + m
            for hh in range(2):
                o_rows.append(o2[hh * HEAD_DIM:(hh + 1) * HEAD_DIM, hh * sub:(hh + 1) * sub])
                lse_rows.append(jnp.broadcast_to(lse2[:, hh * sub:(hh + 1) * sub], (HEAD_DIM, sub)))
        o_ref[0, rs, :] = jnp.concatenate(o_rows, axis=0).T
        lse_ref[0, rs, :] = jnp.concatenate(lse_rows, axis=0).T


def _dil_prompt(q, kv, b, t, dil):
    hw = DIL_HEADS * HEAD_DIM
    l = t // dil
    sub = min(LANES, l)
    has_prev = l > sub
    tq = min(l, 4 * sub)
    per = tq // sub
    q3 = q.reshape(b, l, dil * hw)
    kv3 = kv.reshape(b, l, dil * 2 * hw)
    cur = lambda off: pl.BlockSpec((1, tq, hw), lambda bi, r, i: (bi, i, 2 * r + off))
    prev = lambda off: pl.BlockSpec((1, sub, hw), lambda bi, r, i: (bi, jnp.maximum(i * per - 1, 0), 2 * r + off))
    qo = pl.BlockSpec((1, tq, hw), lambda bi, r, i: (bi, i, r))
    o, lse = pl.pallas_call(
        functools.partial(_dil_prompt_kernel, sub, has_prev),
        grid=(b, dil, l // tq),
        in_specs=[qo, cur(0), cur(1), prev(0), prev(1)],
        out_specs=[qo, qo],
        out_shape=[jax.ShapeDtypeStruct((b, l, dil * hw), F32)] * 2,
        compiler_params=_params(3),
        name="dil_prompt",
    )(q3, kv3, kv3, kv3, kv3)
    return o.reshape(b * t, hw), lse.reshape(b * t, hw)


def _dil_merge_kernel(o0, o1, o2, l0, l1, l2, out_ref):
    ls = [l0[...], l1[...], l2[...]]
    mx = jnp.maximum(jnp.maximum(ls[0], ls[1]), ls[2])
    ws = [jnp.exp(x - mx) for x in ls]
    num = ws[0] * o0[...] + ws[1] * o1[...] + ws[2] * o2[...]
    out_ref[...] = num / (ws[0] + ws[1] + ws[2])


def _dil_merge(outs, lses, tile):
    n, hw = outs[0].shape
    return pl.pallas_call(
        _dil_merge_kernel,
        grid=(n // tile,),
        in_specs=[_rows(hw, tile)] * 6,
        out_specs=_rows(hw, tile),
        out_shape=jax.ShapeDtypeStruct((n, hw), F32),
        compiler_params=_params(1),
        name="dil_merge",
    )(*outs, *lses)


def _snsa_a_kernel(pps, past_len, t_new, pt_ref, *refs):
    pages = refs[:pps]
    qc_ref, pe_ref, phik_ref, phiv_ref, ocmp_ref, idx_ref, cme_ref, cmo_ref = refs[pps:]
    step = pl.program_id(1)
    n_steps = pl.num_programs(1)
    pe = pe_ref[...][None]
    for quad in range(pps // 4):
        x = jnp.concatenate([pages[4 * quad + j][0, 0].T for j in range(4)], axis=0)
        x3 = x.reshape(8, SEL_BLOCK, 256)
        r0 = pl.multiple_of(step * (2 * pps) + quad * 8, 8)
        cme_ref[pl.ds(r0, 8), :] = jnp.mean(x3[:, 0:CMP_BLOCK, :] + pe, axis=1)
        cmo_ref[pl.ds(r0, 8), :] = jnp.mean(x3[:, CMP_BLOCK:SEL_BLOCK, :] + pe, axis=1)

    @pl.when(step == n_steps - 1)
    def _():
        nsb_past = cme_ref.shape[0]
        ncb = 2 * nsb_past
        km = jnp.concatenate([cme_ref[...], cmo_ref[...]], axis=0)
        kc = _dot(km[:, 0:128], phik_ref[...]).astype(BF16)
        vc = _dot(km[:, 128:256], phiv_ref[...]).astype(BF16)
        rows = NSA_GROUP * 8
        t_row = lax.broadcasted_iota(jnp.int32, (rows, 1), 0) & 7
        pos = past_len + t_row
        n_idx = lax.broadcasted_iota(jnp.int32, (rows, ncb), 1)
        n_orig = jnp.where(n_idx < nsb_past, 2 * n_idx, 2 * (n_idx - nsb_past) + 1)
        cmask = (n_orig + 1) * CMP_BLOCK - 1 <= pos
        nsb = -(-(past_len + t_new) // SEL_BLOCK)
        wide = -(-nsb // LANES) * LANES
        blk = lax.broadcasted_iota(jnp.int32, (8, wide), 1)
        blk_f = blk.astype(F32)
        pos8 = past_len + lax.broadcasted_iota(jnp.int32, (8, 1), 0)
        lane = lax.broadcasted_iota(jnp.int32, (8, LANES), 1)
        for g in range(NSA_KV_HEADS):
            s = _dot_nt(qc_ref[0, g], kc)
            (p,), _, _ = _softmax_rows([(s, cmask)])
            ocmp_ref[0, g] = jnp.dot(p.astype(BF16), vc, preferred_element_type=F32)
            imp = sum(p[r * 8:(r + 1) * 8, 0:nsb_past] + p[r * 8:(r + 1) * 8, nsb_past:ncb]
                      for r in range(NSA_GROUP))
            imp = jnp.concatenate([imp, jnp.zeros((8, wide - nsb_past), F32)], axis=1)
            forced = (blk == 0) | (blk == (pos8 >> 6))
            val = jnp.where(forced, jnp.inf, jnp.where(blk * SEL_BLOCK <= pos8, imp, -jnp.inf))
            picks = jnp.full((8, LANES), -1.0, F32)
            for k in range(min(N_SEL, nsb)):
                mx = jnp.max(val, axis=1, keepdims=True)
                first = jnp.min(jnp.where(val == mx, blk_f, 1e9), axis=1, keepdims=True)
                picks = jnp.where(lane == k, jnp.where(mx > -jnp.inf, first, -1.0), picks)
                val = jnp.where(blk_f == first, -jnp.inf, val)
            idx_ref[0, g] = picks.astype(jnp.int32)


def _snsa_a(layer, pt, cache_v, qc_st, pe, phik, phiv, t_new):
    b, n_pages = pt.shape
    past_len = n_pages * LANES
    nsb_past = past_len // SEL_BLOCK
    pps = _pages_per_step(n_pages)
    page = lambda j: pl.BlockSpec(
        (1, 1, 256, LANES), lambda bi, s, pt_ref: (pt_ref[bi, s * pps + j], layer, 0, 0))
    const = lambda shape: pl.BlockSpec(shape, lambda bi, s, pt_ref: (0,) * len(shape))
    grid_spec = pltpu.PrefetchScalarGridSpec(
        num_scalar_prefetch=1,
        grid=(b, n_pages // pps),
        in_specs=[page(j) for j in range(pps)] + [
            pl.BlockSpec((1, NSA_KV_HEADS, NSA_GROUP * 8, LANES), lambda bi, s, pt_ref: (bi, 0, 0, 0)),
            const(pe.shape), const(phik.shape), const(phiv.shape)],
        out_specs=[pl.BlockSpec((1, NSA_KV_HEADS, NSA_GROUP * 8, LANES), lambda bi, s, pt_ref: (bi, 0, 0, 0)),
                   pl.BlockSpec((1, NSA_KV_HEADS, 8, LANES), lambda bi, s, pt_ref: (bi, 0, 0, 0))],
        scratch_shapes=[pltpu.VMEM((nsb_past, 256), F32), pltpu.VMEM((nsb_past, 256), F32)],
    )
    return pl.pallas_call(
        functools.partial(_snsa_a_kernel, pps, past_len, t_new),
        grid_spec=grid_spec,
        out_shape=[jax.ShapeDtypeStruct((b, NSA_KV_HEADS, NSA_GROUP * 8, LANES), F32),
                   jax.ShapeDtypeStruct((b, NSA_KV_HEADS, 8, LANES), jnp.int32)],
        compiler_params=_params(2),
        name="snsa_cmp_topk",
    )(pt, *([cache_v] * pps), qc_st, pe, phik, phiv)


def _snsa_b_kernel(past_len, t_new, n_t, pt_ref, sidx_ref, *refs):
    n_sel = N_SEL
    blocks = refs[:n_sel]
    q_ref, new_ref, o_ref = refs[n_sel:]
    del pt_ref
    bi, ti, gi = pl.program_id(0), pl.program_id(1), pl.program_id(2)
    base = ((bi * n_t + ti) * NSA_KV_HEADS + gi) * n_sel
    new_blk = past_len // SEL_BLOCK
    pos = past_len + ti
    q = q_ref[0, 0, 0].astype(BF16)
    k_new = new_ref[0, 0:128, :]
    v_new = new_ref[0, 128:256, :]
    lane = lax.broadcasted_iota(jnp.int32, (8, LANES), 1)
    pieces, vals = [], []
    for j in range(n_sel):
        idx = sidx_ref[base + j]
        is_new = idx == new_blk
        blk = blocks[j][0, 0]
        k_t = jnp.where(is_new, k_new, blk[0:128]).astype(BF16)
        vals.append(jnp.where(is_new, v_new, blk[128:256]).astype(BF16))
        s = jnp.dot(q, k_t, preferred_element_type=F32)
        idx_v = lane * 0 + idx
        kpos = (idx_v >> 1) * LANES + lane
        in_block = (lane >> 6) == (idx_v & 1)
        pieces.append((s, (idx_v >= 0) & in_block & (kpos <= pos) & (kpos < past_len + t_new)))
    ps, _, _ = _softmax_rows(pieces)
    o_ref[0, 0, 0] = sum(_dot_nt(p, v) for p, v in zip(ps, vals))


def _snsa_b(layer, pt, sidx, cache_t, q_tg, new_blk_t, t_new):
    b, n_pages = pt.shape
    past_len = n_pages * LANES
    n_cache_blk = past_len // SEL_BLOCK

    def blk_spec(j):
        def imap(bi, ti, gi, pt_ref, sidx_ref):
            idx = sidx_ref[((bi * t_new + ti) * NSA_KV_HEADS + gi) * N_SEL + j]
            idx = jnp.clip(idx, 0, n_cache_blk - 1)
            return (pt_ref[bi, idx // 2], layer, 1, 0)
        return pl.BlockSpec((1, 1, 256, LANES), imap)

    grid_spec = pltpu.PrefetchScalarGridSpec(
        num_scalar_prefetch=2,
        grid=(b, t_new, NSA_KV_HEADS),
        in_specs=[blk_spec(j) for j in range(N_SEL)] + [
            pl.BlockSpec((1, 1, 1, 8, LANES), lambda bi, ti, gi, *_: (bi, ti, gi, 0, 0)),
            pl.BlockSpec((1, 256, LANES), lambda bi, ti, gi, *_: (bi, 0, 0))],
        out_specs=pl.BlockSpec((1, 1, 1, 8, LANES), lambda bi, ti, gi, *_: (bi, ti, gi, 0, 0)),
    )
    return pl.pallas_call(
        functools.partial(_snsa_b_kernel, past_len, t_new, t_new),
        grid_spec=grid_spec,
        out_shape=jax.ShapeDtypeStruct((b, t_new, NSA_KV_HEADS, 8, LANES), F32),
        compiler_params=_params(3),
        name="snsa_slc",
    )(pt, sidx, *([cache_t] * N_SEL), q_tg, new_blk_t)


def _snsa_win_kernel(past_len, t_new, q_ref, c_ref, new_ref, o_ref):
    lw = c_ref.shape[3]
    rows = NSA_GROUP * 8
    t_row = lax.broadcasted_iota(jnp.int32, (rows, 1), 0) & 7
    pos = past_len + t_row
    kpos_c = (past_len - lw) + lax.broadcasted_iota(jnp.int32, (rows, lw), 1)
    c_new = lax.broadcasted_iota(jnp.int32, (rows, LANES), 1)
    kpos_n = past_len + c_new
    mask_c = (kpos_c <= pos) & (kpos_c >= pos - (NSA_WINDOW - 1))
    mask_n = (c_new < t_new) & (kpos_n <= pos) & (kpos_n >= pos - (NSA_WINDOW - 1))
    kc, vc = c_ref[0, 0, 0:128, :], c_ref[0, 0, 128:256, :]
    kn, vn = new_ref[0, 0:128, :], new_ref[0, 128:256, :]
    for g in range(NSA_KV_HEADS):
        q = q_ref[0, g]
        (p1, p2), _, _ = _softmax_rows([(_dot(q, kc), mask_c), (_dot(q, kn), mask_n)])
        o_ref[0, g] = _dot_nt(p1, vc) + _dot_nt(p2, vn)


def _snsa_win(layer, qr_st, cache_win, win_new, past_len, t_new):
    b = qr_st.shape[0]
    lw = cache_win.shape[3]
    return pl.pallas_call(
        functools.partial(_snsa_win_kernel, past_len, t_new),
        grid=(b,),
        in_specs=[pl.BlockSpec((1, NSA_KV_HEADS, NSA_GROUP * 8, LANES), lambda i: (i, 0, 0, 0)),
                  pl.BlockSpec((1, 1, 256, lw), lambda i: (layer, i, 0, 0)),
                  pl.BlockSpec((1, 256, LANES), lambda i: (i, 0, 0))],
        out_specs=pl.BlockSpec((1, NSA_KV_HEADS, NSA_GROUP * 8, LANES), lambda i: (i, 0, 0, 0)),
        out_shape=jax.ShapeDtypeStruct((b, NSA_KV_HEADS, NSA_GROUP * 8, LANES), F32),
        compiler_params=_params(1),
        name="snsa_win",
    )(qr_st, cache_win, win_new)


def _scomb_kernel(t_new, oc_ref, os_ref, ow_ref, misc_ref, o_ref):
    gates_t = jax.nn.sigmoid(misc_ref[0].T)
    pad = jnp.zeros((8 - t_new, LANES), F32)
    rows = []
    for g in range(NSA_KV_HEADS):
        gs = slice(g * HEAD_DIM, (g + 1) * HEAD_DIM)
        for r in range(NSA_GROUP):
            h = g * NSA_GROUP + r
            oc = oc_ref[0, g, r * 8:(r + 1) * 8, :].T[gs]
            ow = ow_ref[0, g, r * 8:(r + 1) * 8, :].T[gs]
            os_ = jnp.concatenate([os_ref[0, t, g, r:r + 1, :] for t in range(t_new)] + [pad], axis=0).T[gs]
            row = 32 + h
            rows.append(gates_t[row:row + 1] * oc + gates_t[row + 8:row + 9] * os_ + gates_t[row + 16:row + 17] * ow)
    o_ref[0] = jnp.concatenate(rows, axis=0).T


def _scomb(oc, os_, ow, misc_pad, t_new):
    b = oc.shape[0]
    grp = pl.BlockSpec((1, NSA_KV_HEADS, NSA_GROUP * 8, LANES), lambda i: (i, 0, 0, 0))
    return pl.pallas_call(
        functools.partial(_scomb_kernel, t_new),
        grid=(b,),
        in_specs=[grp, pl.BlockSpec((1, t_new, NSA_KV_HEADS, 8, LANES), lambda i: (i, 0, 0, 0, 0)), grp,
                  pl.BlockSpec((1, 8, LANES), lambda i: (i, 0, 0))],
        out_specs=pl.BlockSpec((1, 8, NSA_HEADS * HEAD_DIM), lambda i: (i, 0, 0)),
        out_shape=jax.ShapeDtypeStruct((b, 8, NSA_HEADS * HEAD_DIM), F32),
        compiler_params=_params(1),
        name="snsa_combine",
    )(oc, os_, ow, misc_pad)


def _col_to_row(col):
    n = col.shape[0]
    return jnp.broadcast_to(col, (n, LANES)).T[0:1, 0:n]


def _smla_kernel(pps, t_new, pt_ref, *refs):
    pages = refs[:pps]
    (qf_ref, wuka_ref, wukb_ref, wuv_ref, new_ref, o_ref,
     qa_ref, qb_ref, m_ref, l_ref, acc_ref) = refs[pps:]
    del pt_ref
    step = pl.program_id(1)
    n_steps = pl.num_programs(1)
    scale = (QK_NOPE + QK_ROPE) ** -0.5
    rows = MLA_HEADS * SMLA_SLOTS
    tail = MLA_ROW - LANES

    @pl.when(step == 0)
    def _():
        for h in range(MLA_HEADS):
            hs = slice(h * SMLA_SLOTS, (h + 1) * SMLA_SLOTS)
            qh = qf_ref[0, :, h * LANES:(h + 1) * LANES].astype(BF16)
            qa_ref[hs, :] = jnp.dot(qh, wuka_ref[h], preferred_element_type=F32)
            qb_ref[hs, :] = jnp.dot(qh, wukb_ref[h], preferred_element_type=F32)
        m_ref[...] = jnp.full(m_ref.shape, NEG, F32)
        l_ref[...] = jnp.zeros(l_ref.shape, F32)
        acc_ref[...] = jnp.zeros(acc_ref.shape, F32)

    qa = qa_ref[...].astype(BF16)
    qb = qb_ref[...].astype(BF16)

    def absorb(x_t, mask):
        s = (jnp.dot(qa, x_t[0:KV_LORA], preferred_element_type=F32)
             + jnp.dot(qb, x_t[tail:MLA_ROW], preferred_element_type=F32)) * scale
        m_old = m_ref[...]
        s_m = s if mask is None else jnp.where(mask, s, NEG)
        m_new = jnp.maximum(m_old, jnp.max(s_m, axis=1, keepdims=True))
        p = jnp.exp(s - m_new)
        if mask is not None:
            p = jnp.where(mask, p, 0.0)
        alpha = jnp.exp(m_old - m_new)
        l_ref[...] = alpha * l_ref[...] + jnp.sum(p, axis=1, keepdims=True)
        acc_ref[...] = (acc_ref[...] * _col_to_row(alpha)
                        + jnp.dot(x_t, p.T.astype(BF16), preferred_element_type=F32))
        m_ref[...] = m_new

    absorb(jnp.concatenate([pages[j][0, 0].astype(BF16) for j in range(pps)], axis=1), None)

    @pl.when(step == n_steps - 1)
    def _():
        c = lax.broadcasted_iota(jnp.int32, (rows, LANES), 1)
        t_row = lax.broadcasted_iota(jnp.int32, (rows, LANES), 0) & (SMLA_SLOTS - 1)
        absorb(new_ref[0].astype(BF16), (c < t_new) & (c <= t_row))
        o_lat = (acc_ref[0:KV_LORA, :] * _col_to_row(1.0 / l_ref[...])).T.astype(BF16)
        outs = [jnp.dot(o_lat[h * SMLA_SLOTS:(h + 1) * SMLA_SLOTS], wuv_ref[h], preferred_element_type=F32)
                for h in range(MLA_HEADS)]
        o_ref[0] = jnp.concatenate([jnp.concatenate(outs[2 * j:2 * j + 2], axis=1)
                                    for j in range(MLA_HEADS // 2)], axis=1)


def _smla(layer, pt, cache_t, qf_pad, wuka, wukb, wuv, new_t, t_new):
    b, n_pages = pt.shape
    pps = _pages_per_step(n_pages)
    page = lambda j: pl.BlockSpec(
        (1, 1, MLA_ROW, LANES), lambda bi, s, pt_ref: (pt_ref[bi, s * pps + j], layer, 0, 0))
    const = lambda shape: pl.BlockSpec(shape, lambda bi, s, pt_ref: (0,) * len(shape))
    rows = MLA_HEADS * SMLA_SLOTS
    grid_spec = pltpu.PrefetchScalarGridSpec(
        num_scalar_prefetch=1,
        grid=(b, n_pages // pps),
        in_specs=[page(j) for j in range(pps)] + [
            pl.BlockSpec((1, SMLA_SLOTS, MLA_HEADS * LANES), lambda bi, s, pt_ref: (bi, 0, 0)),
            const(wuka.shape), const(wukb.shape), const(wuv.shape),
            pl.BlockSpec((1, MLA_ROW, LANES), lambda bi, s, pt_ref: (bi, 0, 0))],
        out_specs=pl.BlockSpec((1, SMLA_SLOTS, MLA_HEADS * V_DIM), lambda bi, s, pt_ref: (bi, 0, 0)),
        scratch_shapes=[pltpu.VMEM((rows, KV_LORA), F32), pltpu.VMEM((rows, LANES), F32),
                        pltpu.VMEM((rows, 1), F32), pltpu.VMEM((rows, 1), F32),
                        pltpu.VMEM((MLA_ROW, rows), F32)],
    )
    return pl.pallas_call(
        functools.partial(_smla_kernel, pps, t_new),
        grid_spec=grid_spec,
        out_shape=jax.ShapeDtypeStruct((b, SMLA_SLOTS, MLA_HEADS * V_DIM), F32),
        compiler_params=_params(2),
        name="smla",
    )(pt, *([cache_t] * pps), qf_pad, wuka, wukb, wuv, new_t)


def _sdil_kernel(t_new, q_ref, new_ref, c0_ref, c1_ref, c2_ref, o_ref):
    hw = DIL_HEADS * HEAD_DIM
    rows = DIL_HEADS * 8
    r_idx = lax.broadcasted_iota(jnp.int32, (rows, hw), 0)
    l_idx = lax.broadcasted_iota(jnp.int32, (rows, hw), 1)
    head_mask = jnp.where((r_idx >> 3) == (l_idx >> 6), 1.0, 0.0)
    outs, lses = [], []
    for g, (c_ref, (window, dil)) in enumerate(zip((c0_ref, c1_ref, c2_ref), DIL_PATTERNS)):
        lb = c_ref.shape[3]
        q = q_ref[0, :, g * hw:(g + 1) * hw]
        qs = jnp.concatenate([q] * DIL_HEADS, axis=0) * head_mask
        kc, vc = c_ref[0, 0, 0:hw, :], c_ref[0, 0, hw:2 * hw, :]
        kn = new_ref[0, g * 2 * hw:g * 2 * hw + hw, :]
        vn = new_ref[0, g * 2 * hw + hw:(g + 1) * 2 * hw, :]
        t_c = lax.broadcasted_iota(jnp.int32, (rows, lb), 0) & 7
        d_c = lb + t_c - lax.broadcasted_iota(jnp.int32, (rows, lb), 1)
        mask_c = (d_c >= 0) & ((d_c & (dil - 1)) == 0) & (d_c <= window)
        t_n = lax.broadcasted_iota(jnp.int32, (rows, LANES), 0) & 7
        c_n = lax.broadcasted_iota(jnp.int32, (rows, LANES), 1)
        d_n = t_n - c_n
        mask_n = (c_n < t_new) & (d_n >= 0) & ((d_n & (dil - 1)) == 0) & (d_n <= window)
        (p1, p2), den, m = _softmax_rows([(_dot(qs, kc), mask_c), (_dot(qs, kn), mask_n)])
        o = (_dot_nt(p1, vc) + _dot_nt(p2, vn)) * head_mask
        outs.append(jnp.sum(o.reshape(DIL_HEADS, 8, hw), axis=0))
        lse = (jnp.log(den) + m) * head_mask
        lses.append(jnp.sum(lse.reshape(DIL_HEADS, 8, hw), axis=0))
    mx = jnp.maximum(jnp.maximum(lses[0], lses[1]), lses[2])
    ws = [jnp.exp(x - mx) for x in lses]
    o_ref[0] = (ws[0] * outs[0] + ws[1] * outs[1] + ws[2] * outs[2]) / (ws[0] + ws[1] + ws[2])


def _sdil(layer, q_pad, new_pad, caches, t_new):
    b = q_pad.shape[0]
    hw = DIL_HEADS * HEAD_DIM
    cspec = lambda c: pl.BlockSpec((1, 1, 2 * hw, c.shape[3]), lambda i: (layer, i, 0, 0))
    return pl.pallas_call(
        functools.partial(_sdil_kernel, t_new),
        grid=(b,),
        in_specs=[pl.BlockSpec((1, 8, N_DIL * hw), lambda i: (i, 0, 0)),
                  pl.BlockSpec((1, N_DIL * 2 * hw, LANES), lambda i: (i, 0, 0))] + [cspec(c) for c in caches],
        out_specs=pl.BlockSpec((1, 8, hw), lambda i: (i, 0, 0)),
        out_shape=jax.ShapeDtypeStruct((b, 8, hw), F32),
        compiler_params=_params(1),
        name="sdil",
    )(q_pad, new_pad, *caches)


def _pack_w_ab(w):
    q_a, kv_a, g_a, q_lat, kv_lat, kpe = jnp.split(w, np.cumsum(AB_SPLITS)[:-1].tolist(), axis=1)
    d = w.shape[0]
    qh = q_a.reshape(d, NSA_HEADS, HEAD_DIM)
    zero = jnp.zeros((d, HEAD_DIM), w.dtype)
    chunks = []
    for h in range(NSA_HEADS):
        pair = [qh[:, h], zero] if h // NSA_GROUP == 0 else [zero, qh[:, h]]
        chunks.append(jnp.concatenate(pair, axis=1))
    misc = jnp.concatenate([kpe, g_a, jnp.zeros((d, LANES - QK_ROPE - 3 * NSA_HEADS), w.dtype)], axis=1)
    return jnp.concatenate(chunks + [kv_a, q_lat, kv_lat, misc], axis=1).astype(BF16)


def _pack_w_uq(w):
    wh = w.reshape(Q_LORA, MLA_HEADS, QK_NOPE + QK_ROPE)
    pad = jnp.zeros((Q_LORA, MLA_HEADS, LANES - QK_NOPE - QK_ROPE), w.dtype)
    return jnp.concatenate([wh[..., QK_NOPE:], pad, wh[..., :QK_NOPE]], axis=-1).reshape(
        Q_LORA, MLA_HEADS * LANES).astype(BF16)


def _pack_w_kvup(w_uk, w_uv):
    k_top = jnp.concatenate([jnp.zeros((KV_LORA, MLA_HEADS, LANES - QK_NOPE), w_uk.dtype), w_uk], axis=-1)
    eye = jnp.eye(LANES, dtype=w_uk.dtype) * (np.arange(LANES) < QK_ROPE)[:, None]
    k_bot = jnp.broadcast_to(eye[:, None, :], (LANES, MLA_HEADS, LANES))
    k_part = jnp.concatenate([k_top, k_bot], axis=0).reshape(KV_LORA + LANES, MLA_HEADS * LANES)
    v_part = jnp.concatenate([w_uv.reshape(KV_LORA, MLA_HEADS * V_DIM),
                              jnp.zeros((LANES, MLA_HEADS * V_DIM), w_uv.dtype)], axis=0)
    return jnp.concatenate([k_part, v_part], axis=1).astype(BF16)


def _pack_w_absorb(w_uk, w_uv):
    zero = jnp.zeros((MLA_HEADS, LANES - QK_NOPE, KV_LORA), w_uk.dtype)
    up_a = jnp.concatenate([zero, jnp.transpose(w_uk, (1, 2, 0))], axis=1)
    place = np.zeros((LANES, LANES), np.float32)
    place[np.arange(QK_ROPE), LANES - QK_ROPE + np.arange(QK_ROPE)] = 1.0
    up_b = jnp.broadcast_to(jnp.asarray(place)[None], (MLA_HEADS, LANES, LANES))
    down = jnp.transpose(w_uv, (1, 0, 2))
    return up_a.astype(BF16), up_b.astype(BF16), down.astype(BF16)


def _block_diag2(phi):
    z = jnp.zeros((HEAD_DIM, HEAD_DIM), phi.dtype)
    return jnp.concatenate([jnp.concatenate([phi[0], z], axis=1),
                            jnp.concatenate([z, phi[1]], axis=1)], axis=0).astype(BF16)


def _pad_rows(x, rows):
    return jnp.pad(x, ((0, 0), (0, rows - x.shape[1]), (0, 0)))


def _stack_group_rows(q, b, t):
    q5 = q.reshape(b, t, NSA_KV_HEADS, NSA_GROUP, LANES)
    q5 = jnp.pad(jnp.transpose(q5, (0, 2, 3, 1, 4)), ((0, 0), (0, 0), (0, 0), (0, 8 - t), (0, 0)))
    return q5.reshape(b, NSA_KV_HEADS, NSA_GROUP * 8, LANES)


def kernel(x_prompt, x_sample, cache_nsa_kv, cache_mla, cache_nsa_win, cache_dil0, cache_dil1, cache_dil2,
           page_table, p_prompt, p_sample, w_in_ab, w_out_ab, nsa_pe_k, nsa_pe_v, nsa_phi_k, nsa_phi_v,
           mla_q_norm, mla_kv_norm, mla_w_uq, mla_w_uk, mla_w_uv, w_in_c, w_out_c, norm_mix, norm_ffn,
           w_gate_up, w_down, norm_ple, w_ple_gate, w_ple_proj, norm_final):
    bp, tp, _ = x_prompt.shape
    bs, ts, _ = x_sample.shape
    depth = norm_mix.shape[0]
    n_ab = w_in_ab.shape[0]
    n_pool, page_size = cache_nsa_kv.shape[:2]
    n_pages = page_table.shape[1]
    past_len = n_pages * page_size
    assert page_size == LANES and tp % (2 * SEL_BLOCK) == 0 and ts <= 8 and past_len % SEL_BLOCK == 0
    assert MLA_TQ == MLA_TK
    np_, ns_ = bp * tp, bs * ts
    hw = DIL_HEADS * HEAD_DIM

    pos_p = jnp.arange(tp, dtype=jnp.int32)
    pos_s = past_len + jnp.arange(ns_, dtype=jnp.int32) % ts
    tile_p = min(ROW_TILE, tp)
    tile_s = ns_
    tabs = {
        "p": (_rope_tables(pos_p, ROT_DIM, HEAD_DIM, 0), _rope_tables(pos_p, QK_ROPE, LANES, 0), tile_p),
        "s": (_rope_tables(pos_s, ROT_DIM, HEAD_DIM, 0), _rope_tables(pos_s, QK_ROPE, LANES, 0), tile_s),
    }
    expand = jnp.asarray(np.arange(tp)[:, None] // SEL_BLOCK == np.arange(tp // SEL_BLOCK)[None], BF16)

    def rows_last(c, feat):
        perm = tuple(i for i in range(c.ndim) if i != feat) + (feat,)
        return jnp.transpose(c, perm)

    cache_nsa_t = rows_last(cache_nsa_kv, 1).reshape(n_pool, n_ab, 512, page_size)
    cache_mla_t = rows_last(cache_mla, 1)
    cache_win_t = rows_last(cache_nsa_win, 2).reshape(n_ab, bs, 256, -1)
    caches_dil_t = [rows_last(c, 2).reshape(c.shape[0], bs, 2 * hw, -1) for c in (cache_dil0, cache_dil1, cache_dil2)]
    slots_last = lambda x: jnp.transpose(x, (0, 2, 1))

    row2 = lambda v: v.reshape(1, -1)
    h_p = x_prompt.reshape(np_, D_MODEL)
    h_s = x_sample.reshape(ns_, D_MODEL)
    nsa_p, nsa_s, mla_p, mla_s, win_p, win_s = [], [], [], [], [], []
    dil_p = [[] for _ in DIL_PATTERNS]
    dil_s = [[] for _ in DIL_PATTERNS]
    y_p = y_s = None

    for i in range(depth):
        l = i // 2
        final = i == depth - 1
        gm = row2(norm_mix[i])
        if i % 2 == 0:
            w_ab = _pack_w_ab(w_in_ab[l])
            w_uq = _pack_w_uq(mla_w_uq[l])
            gq, gkv = row2(mla_q_norm[l]), row2(mla_kv_norm[l])
            pe = jnp.concatenate([nsa_pe_k[l].reshape(CMP_BLOCK, 128), nsa_pe_v[l].reshape(CMP_BLOCK, 128)], axis=1)
            phik, phiv = _block_diag2(nsa_phi_k[l]), _block_diag2(nsa_phi_v[l])
            w_out = w_out_ab[l].astype(BF16)

            t64, tm_, tile = tabs["p"]
            qc, qr, rows, win, qf, mla, misc = _pre_ab(h_p, gm, w_ab, gq, gkv, w_uq, t64, tm_, tile)
            rows3 = rows.reshape(bp, tp, 512)
            kc, vc = _cmp_prompt(rows3, pe, phik, phiv)
            o_a = _nsa_prompt(qc.reshape(bp, tp, -1), qr.reshape(bp, tp, -1), rows3, win.reshape(bp, tp, 256),
                              kc, vc, misc.reshape(bp, tp, LANES), expand)
            kf, vv = _mla_kv(mla, misc, _pack_w_kvup(mla_w_uk[l], mla_w_uv[l]), tile)
            o_b = _mla_prompt(qf.reshape(bp, tp, -1), kf.reshape(bp, tp, -1), vv.reshape(bp, tp, -1))
            mix_p = [o_a.reshape(np_, -1), o_b.reshape(np_, -1)]
            nsa_p.append(rows3)
            mla_p.append(mla.reshape(bp, tp, MLA_ROW))
            win_p.append(win.reshape(bp, tp, 256)[:, max(tp - NSA_WINDOW, 0):])

            t64, tm_, tile = tabs["s"]
            qc, qr, rows, win, qf, mla, misc = _pre_ab(h_s, gm, w_ab, gq, gkv, w_uq, t64, tm_, tile)
            o_cmp, sel_idx = _snsa_a(l, page_table, cache_nsa_t, _stack_group_rows(qc, bs, ts),
                                     pe, phik, phiv, ts)
            sidx = jnp.transpose(sel_idx[:, :, :ts, :N_SEL], (0, 2, 1, 3)).reshape(-1)
            q_tg = jnp.pad(qr.reshape(bs, ts, NSA_KV_HEADS, NSA_GROUP, LANES),
                           ((0, 0), (0, 0), (0, 0), (0, 8 - NSA_GROUP), (0, 0)))
            new_blk = slots_last(_pad_rows(rows.reshape(bs, ts, 512)[:, :, 256:512], LANES))
            o_slc = _snsa_b(l, page_table, sidx, cache_nsa_t, q_tg, new_blk, ts)
            o_win = _snsa_win(l, _stack_group_rows(qr, bs, ts), cache_win_t,
                              slots_last(_pad_rows(win.reshape(bs, ts, 256), LANES)), past_len, ts)
            o_a = _scomb(o_cmp, o_slc, o_win, _pad_rows(misc.reshape(bs, ts, LANES), 8), ts)
            o_a = o_a[:, :ts].reshape(ns_, NSA_HEADS * HEAD_DIM)
            mla_new = slots_last(_pad_rows(mla.reshape(bs, ts, MLA_ROW), LANES))
            o_b = _smla(l, page_table, cache_mla_t, _pad_rows(qf.reshape(bs, ts, -1), SMLA_SLOTS),
                        *_pack_w_absorb(mla_w_uk[l], mla_w_uv[l]), mla_new, ts)
            mix_s = [o_a, o_b[:, :ts].reshape(ns_, -1)]
            nsa_s.append(rows.reshape(bs, ts, 512))
            mla_s.append(mla.reshape(bs, ts, MLA_ROW))
            win_s.append(win.reshape(bs, ts, 256))
        else:
            w_c = w_in_c[l].astype(BF16)
            w_out = w_out_c[l].astype(BF16)

            t64, _, tile = tabs["p"]
            outs = _pre_c(h_p, gm, w_c, t64, tile)
            os_, ls_ = [], []
            for g, (window, dil) in enumerate(DIL_PATTERNS):
                o, lse = _dil_prompt(outs[g], outs[N_DIL + g], bp, tp, dil)
                os_.append(o)
                ls_.append(lse)
                dil_p[g].append(outs[N_DIL + g].reshape(bp, tp, 2 * hw)[:, max(tp - window, 0):])
            mix_p = [_dil_merge(os_, ls_, tile)]

            t64, _, tile = tabs["s"]
            outs = _pre_c(h_s, gm, w_c, t64, tile)
            q_pad = _pad_rows(jnp.concatenate(outs[:N_DIL], axis=1).reshape(bs, ts, -1), 8)
            new_pad = slots_last(_pad_rows(jnp.concatenate(outs[N_DIL:], axis=1).reshape(bs, ts, -1), LANES))
            mix_s = [_sdil(l, q_pad, new_pad, caches_dil_t, ts)[:, :ts].reshape(ns_, hw)]
            for g in range(N_DIL):
                dil_s[g].append(outs[N_DIL + g].reshape(bs, ts, 2 * hw))

        post_w = (w_out, row2(norm_ffn[i]), w_gate_up[i].astype(BF16), w_down[i].astype(BF16),
                  row2(norm_ple[i]), w_ple_gate[i].astype(BF16))
        wpp, gfin = w_ple_proj[i].astype(BF16), row2(norm_final)
        res = _post(h_p, mix_p, *post_w, p_prompt[i].reshape(np_, PLE_DIM), wpp, gfin, final, min(POST_TILE, tp))
        h_p = res[0]
        if final:
            y_p = res[1]
        res = _post(h_s, mix_s, *post_w, p_sample[i].reshape(ns_, PLE_DIM), wpp, gfin, final, tabs["s"][2])
        h_s = res[0]
        if final:
            y_s = res[1]

    def rows_out(parts, b, t, tail):
        return jnp.stack(parts, axis=2).reshape((b, t, len(parts)) + tail)

    def bufs_out(parts, tail):
        x = jnp.stack(parts, axis=0)
        return x.reshape(x.shape[:3] + tail)

    kv_tail = (4, NSA_KV_HEADS, HEAD_DIM)
    win_tail = (2, NSA_KV_HEADS, HEAD_DIM)
    dil_tail = (2, DIL_HEADS, HEAD_DIM)
    out = [y_p.reshape(bp, tp, D_MODEL), y_s.reshape(bs, ts, D_MODEL),
           rows_out(nsa_p, bp, tp, kv_tail), rows_out(nsa_s, bs, ts, kv_tail),
           rows_out(mla_p, bp, tp, (MLA_ROW,)), rows_out(mla_s, bs, ts, (MLA_ROW,)),
           bufs_out(win_p, win_tail), bufs_out(win_s, win_tail)]
    for g in range(N_DIL):
        out += [bufs_out(dil_p[g], dil_tail), bufs_out(dil_s[g], dil_tail)]
    return tuple(out)
```
